```python
import jax, jax.numpy as jnp
from jax import lax
import numpy as np

D_MODEL = 1024
BATCH = 8
SEQ = 2048
DEPTH = 1
DEC_BATCH = 128
DEC_SEQ = 8
PAST_LEN = 16384
PAGE_SIZE = 128

D_A = D_MODEL
D_B = D_MODEL
K_A = 3
K_B = 31
PLE_DIM = 256
EPS = 1e-6
LN_EPS = 1e-5
SPLITS = [D_A, 2 * D_A, 3 * D_A, 4 * D_A,
          4 * D_A + D_B, 4 * D_A + 2 * D_B, 4 * D_A + 3 * D_B,
          4 * D_A + 3 * D_B + D_MODEL]
D_IN = 4 * D_A + 3 * D_B + 2 * D_MODEL

kernel_name = "hybrid_shortconv_conformer_conv_decoder_step"


def _rmsnorm(x, g):
    xf = x.astype(jnp.float32)
    y = xf * lax.rsqrt(jnp.mean(xf * xf, axis=-1, keepdims=True) + EPS)
    return (y * g.astype(jnp.float32)).astype(x.dtype)


def _layernorm(x, g, b):
    xf = x.astype(jnp.float32)
    mu = jnp.mean(xf, axis=-1, keepdims=True)
    xc = xf - mu
    var = jnp.mean(xc * xc, axis=-1, keepdims=True)
    y = xc * lax.rsqrt(var + LN_EPS) * g.astype(jnp.float32) + b.astype(jnp.float32)
    return y.astype(x.dtype)


def _causal_dwconv(x_ext, w):
    c = x_ext.shape[-1]
    return lax.conv_general_dilated(
        x_ext, w[:, None, :].astype(x_ext.dtype), window_strides=(1,), padding='VALID',
        dimension_numbers=('NWC', 'WIO', 'NWC'), feature_group_count=c)


def _layer(h, p, buf_a, buf_b, g_mix, w_in, w_conv_a, w_out_a, w_conv_b, b_conv_b,
           ln_g, ln_b, w_out_b, w_o, w_pe, g_ple, w_pg):
    u = _rmsnorm(h, g_mix)
    proj = jnp.einsum('ntd,de->nte', u, w_in)
    a_b, a_c, a_h, a_z, b_v, b_g, b_z, gate_a, gate_b = jnp.split(proj, SPLITS, axis=-1)

    s = a_c * a_h
    ext_a = jnp.concatenate([buf_a.astype(s.dtype), s], axis=1)
    ya = a_b * _causal_dwconv(ext_a, w_conv_a) * jax.nn.silu(a_z)
    ya = jnp.einsum('ntc,cd->ntd', ya, w_out_a)
    new_buf_a = ext_a[:, -(K_A - 1):]

    v = b_v * jax.nn.sigmoid(b_g)
    ext_b = jnp.concatenate([buf_b.astype(v.dtype), v], axis=1)
    c = _causal_dwconv(ext_b, w_conv_b) + b_conv_b
    yb = jax.nn.silu(_layernorm(c, ln_g, ln_b)) * jax.nn.silu(b_z)
    yb = jnp.einsum('ntc,cd->ntd', yb, w_out_b)
    new_buf_b = ext_b[:, -(K_B - 1):]

    m = jax.nn.sigmoid(gate_a) * ya + jax.nn.sigmoid(gate_b) * yb
    h = h + jnp.einsum('ntd,de->nte', m, w_o)

    pe = jnp.einsum('ntp,pd->ntd', p.astype(h.dtype), w_pe)
    pg = jax.nn.sigmoid(jnp.einsum('ntd,de->nte', _rmsnorm(h, g_ple), w_pg))
    h = h + pg * pe
    return h, new_buf_a, new_buf_b


def setup_inputs(seed: int = 0) -> dict:
    key = jax.random.key(seed)
    ks = jax.random.split(key, 24)
    f32 = jnp.float32

    def nrm(k, shape, scale):
        return jax.random.normal(k, shape, f32) * scale

    return {
        "x_prompt": nrm(ks[0], (BATCH, SEQ, D_MODEL), 1.0),
        "x_sample": nrm(ks[1], (DEC_BATCH, DEC_SEQ, D_MODEL), 1.0),
        "state_conv_a": nrm(ks[2], (DEPTH, DEC_BATCH, K_A - 1, D_A), 1.0),
        "state_conv_b": nrm(ks[3], (DEPTH, DEC_BATCH, K_B - 1, D_B), 0.5),
        "p_prompt": nrm(ks[4], (DEPTH, BATCH, SEQ, PLE_DIM), 1.0),
        "p_sample": nrm(ks[5], (DEPTH, DEC_BATCH, DEC_SEQ, PLE_DIM), 1.0),
        "g_mix": 1.0 + nrm(ks[6], (DEPTH, D_MODEL), 0.02),
        "w_in": nrm(ks[7], (DEPTH, D_MODEL, D_IN), D_MODEL ** -0.5),
        "w_conv_a": nrm(ks[8], (DEPTH, K_A, D_A), K_A ** -0.5),
        "w_out_a": nrm(ks[9], (DEPTH, D_A, D_MODEL), D_A ** -0.5),
        "w_conv_b": nrm(ks[10], (DEPTH, K_B, D_B), K_B ** -0.5),
        "b_conv_b": nrm(ks[11], (DEPTH, D_B), 0.02),
        "ln_g": 1.0 + nrm(ks[12], (DEPTH, D_B), 0.02),
        "ln_b": nrm(ks[13], (DEPTH, D_B), 0.02),
        "w_out_b": nrm(ks[14], (DEPTH, D_B, D_MODEL), D_B ** -0.5),
        "w_o": nrm(ks[15], (DEPTH, D_MODEL, D_MODEL), D_MODEL ** -0.5),
        "w_pe": nrm(ks[16], (DEPTH, PLE_DIM, D_MODEL), PLE_DIM ** -0.5),
        "g_ple": 1.0 + nrm(ks[17], (DEPTH, D_MODEL), 0.02),
        "w_pg": nrm(ks[18], (DEPTH, D_MODEL, D_MODEL), D_MODEL ** -0.5),
        "g_final": 1.0 + nrm(ks[19], (D_MODEL,), 0.02),
    }


def reference(x_prompt, x_sample, state_conv_a, state_conv_b, p_prompt, p_sample,
              g_mix, w_in, w_conv_a, w_out_a, w_conv_b, b_conv_b, ln_g, ln_b,
              w_out_b, w_o, w_pe, g_ple, w_pg, g_final):
    n_p = x_prompt.shape[0]
    hp, hs = x_prompt, x_sample
    na_p, nb_p, na_s, nb_s = [], [], [], []
    for i in range(DEPTH):
        lw = (g_mix[i], w_in[i], w_conv_a[i], w_out_a[i], w_conv_b[i], b_conv_b[i],
              ln_g[i], ln_b[i], w_out_b[i], w_o[i], w_pe[i], g_ple[i], w_pg[i])
        za = jnp.zeros((n_p, K_A - 1, D_A), hp.dtype)
        zb = jnp.zeros((n_p, K_B - 1, D_B), hp.dtype)
        hp, ba, bb = _layer(hp, p_prompt[i], za, zb, *lw)
        na_p.append(ba)
        nb_p.append(bb)
        hs, sa, sb = _layer(hs, p_sample[i], state_conv_a[i], state_conv_b[i], *lw)
        na_s.append(sa)
        nb_s.append(sb)
    y_prompt = _rmsnorm(hp, g_final)
    y_sample = _rmsnorm(hs, g_final)
    return (y_prompt, y_sample, jnp.stack(na_p), jnp.stack(nb_p), jnp.stack(na_s), jnp.stack(nb_s))
```

```python
import functools

import jax
import jax.numpy as jnp
from jax import lax
from jax.experimental import pallas as pl
from jax.experimental.pallas import tpu as pltpu

D = 1024
N_PROJ = 9
PLE = 256
K_A = 3
K_B = 31
EPS = 1e-6
LN_EPS = 1e-5
LANES = 128
SUBLANES = 8
N_SLAB = D // LANES
HIST_A = SUBLANES
HIST_B = -(-(K_B - 1) // SUBLANES) * SUBLANES
ROW_BLOCK = 64
VMEM_LIMIT_BYTES = 58 * 1024 * 1024

P_AB, P_AC, P_AH, P_AZ, P_BV, P_BG, P_BZ, P_GA, P_GB = range(N_PROJ)


def _sigmoid(x):
    return 0.5 * jnp.tanh(0.5 * x) + 0.5


def _silu(x):
    h = 0.5 * x
    return h * jnp.tanh(h) + h


def _rmsnorm(x, g):
    ms = jnp.mean(x * x, axis=-1, keepdims=True)
    return x * lax.rsqrt(ms + EPS) * g


def _layernorm(x, g, b):
    mu = jnp.mean(x, axis=-1, keepdims=True)
    xc = x - mu
    var = jnp.mean(xc * xc, axis=-1, keepdims=True)
    return xc * lax.rsqrt(var + LN_EPS) * g + b


def _to_slabs(slab_ref, val, nseq, t_len, hist):
    pitch = hist + t_len
    for q in range(nseq):
        for c in range(N_SLAB):
            slab_ref[c, q * pitch + hist:(q + 1) * pitch, :] = (
                val[q * t_len:(q + 1) * t_len, c * LANES:(c + 1) * LANES])


def _dwconv(slab_ref, w_ref, out_ref, nseq, t_len, hist, k_taps):
    pitch = hist + t_len
    first = hist - (k_taps - 1)
    rb = min(ROW_BLOCK, t_len)
    for q in range(nseq):
        for c in range(N_SLAB):
            lanes = slice(c * LANES, (c + 1) * LANES)
            for r0 in range(0, t_len, rb):
                acc = None
                for k in range(k_taps):
                    start = q * pitch + first + r0 + k
                    term = slab_ref[c, start:start + rb, :] * w_ref[k:k + 1, lanes]
                    acc = term if acc is None else acc + term
                out_ref[q * t_len + r0:q * t_len + r0 + rb, lanes] = acc


def _layer_tile(x_ref, p_ref, w_in, w_out_a, w_out_b, w_o, w_pe, w_pg, w_conv_a, w_conv_b,
                g_mix, b_conv_b, ln_g, ln_b, g_ple, g_final,
                slab_a, slab_b, conv_a_buf, conv_b_buf, y_ref, nseq, t_len):
    f32, bf16 = jnp.float32, jnp.bfloat16
    x = x_ref[...]
    u = _rmsnorm(x, g_mix[...]).astype(bf16)

    def proj(j):
        return jnp.dot(u, w_in[:, j * D:(j + 1) * D], preferred_element_type=f32)

    v = proj(P_BV) * _sigmoid(proj(P_BG))
    _to_slabs(slab_b, v, nseq, t_len, HIST_B)
    s = proj(P_AC) * proj(P_AH)
    _to_slabs(slab_a, s, nseq, t_len, HIST_A)

    _dwconv(slab_b, w_conv_b, conv_b_buf, nseq, t_len, HIST_B, K_B)
    cb = conv_b_buf[...] + b_conv_b[...]
    yb = _silu(_layernorm(cb, ln_g[...], ln_b[...])) * _silu(proj(P_BZ))
    yb = jnp.dot(yb.astype(bf16), w_out_b[...], preferred_element_type=f32)

    _dwconv(slab_a, w_conv_a, conv_a_buf, nseq, t_len, HIST_A, K_A)
    ya = proj(P_AB) * conv_a_buf[...] * _silu(proj(P_AZ))
    ya = jnp.dot(ya.astype(bf16), w_out_a[...], preferred_element_type=f32)

    m = _sigmoid(proj(P_GA)) * ya + _sigmoid(proj(P_GB)) * yb
    h = x + jnp.dot(m.astype(bf16), w_o[...], preferred_element_type=f32)

    pe = jnp.dot(p_ref[...].astype(bf16), w_pe[...], preferred_element_type=f32)
    r = _rmsnorm(h, g_ple[...]).astype(bf16)
    pg = _sigmoid(jnp.dot(r, w_pg[...], preferred_element_type=f32))
    h = h + pg * pe
    y_ref[...] = _rmsnorm(h, g_final[...])
    return s, v


def _prompt_kernel(x_ref, p_ref, w_in, w_out_a, w_out_b, w_o, w_pe, w_pg, w_conv_a, w_conv_b,
                   g_mix, b_conv_b, ln_g, ln_b, g_ple, g_final,
                   y_ref, tail_a_ref, tail_b_ref,
                   slab_a, slab_b, conv_a_buf, conv_b_buf, *, tm):
    t = pl.program_id(1)

    @pl.when(t == 0)
    def _():
        slab_a[:, 0:HIST_A, :] = jnp.zeros((N_SLAB, HIST_A, LANES), jnp.float32)
        slab_b[:, 0:HIST_B, :] = jnp.zeros((N_SLAB, HIST_B, LANES), jnp.float32)

    @pl.when(t > 0)
    def _():
        slab_a[:, 0:HIST_A, :] = slab_a[:, tm:tm + HIST_A, :]
        slab_b[:, 0:HIST_B, :] = slab_b[:, tm:tm + HIST_B, :]

    s, v = _layer_tile(x_ref, p_ref, w_in, w_out_a, w_out_b, w_o, w_pe, w_pg, w_conv_a, w_conv_b,
                       g_mix, b_conv_b, ln_g, ln_b, g_ple, g_final,
                       slab_a, slab_b, conv_a_buf, conv_b_buf, y_ref, 1, tm)

    @pl.when(t == pl.num_programs(1) - 1)
    def _():
        tail_a_ref[...] = s[tm - HIST_A:, :]
        tail_b_ref[...] = v[tm - HIST_B:, :]


def _sample_kernel(x_ref, p_ref, st_a_ref, st_b_ref,
                   w_in, w_out_a, w_out_b, w_o, w_pe, w_pg, w_conv_a, w_conv_b,
                   g_mix, b_conv_b, ln_g, ln_b, g_ple, g_final,
                   y_ref, s_ref, v_ref,
                   slab_a, slab_b, conv_a_buf, conv_b_buf, *, nseq, t_len):
    pitch_a = HIST_A + t_len
    pitch_b = HIST_B + t_len
    for q in range(nseq):
        for c in range(N_SLAB):
            lanes = slice(c * LANES, (c + 1) * LANES)
            slab_a[c, q * pitch_a:q * pitch_a + HIST_A, :] = st_a_ref[q, :, lanes]
            slab_b[c, q * pitch_b:q * pitch_b + HIST_B, :] = st_b_ref[q, :, lanes]
    s, v = _layer_tile(x_ref, p_ref, w_in, w_out_a, w_out_b, w_o, w_pe, w_pg, w_conv_a, w_conv_b,
                       g_mix, b_conv_b, ln_g, ln_b, g_ple, g_final,
                       slab_a, slab_b, conv_a_buf, conv_b_buf, y_ref, nseq, t_len)
    s_ref[...] = s
    v_ref[...] = v


def _whole(memory_space=pltpu.VMEM):
    return pl.BlockSpec(memory_space=memory_space)


def kernel(x_prompt, x_sample, state_conv_a, state_conv_b, p_prompt, p_sample, g_mix, w_in, w_conv_a, w_out_a, w_conv_b, b_conv_b, ln_g, ln_b, w_out_b, w_o, w_pe, g_ple, w_pg, g_final):
    f32, bf16 = jnp.float32, jnp.bfloat16
    depth = w_in.shape[0]
    assert depth == 1, "single-layer trunk"
    n_p, seq, d = x_prompt.shape
    n_s, t_s, _ = x_sample.shape
    assert d == D and w_in.shape[2] == N_PROJ * D

    weights = (w_in[0].astype(bf16), w_out_a[0].astype(bf16), w_out_b[0].astype(bf16),
               w_o[0].astype(bf16), w_pe[0].astype(bf16), w_pg[0].astype(bf16),
               w_conv_a[0], w_conv_b[0])
    vecs = (g_mix[0][None], b_conv_b[0][None], ln_g[0][None], ln_b[0][None],
            g_ple[0][None], g_final[None])
    const_specs = [_whole() for _ in range(len(weights) + len(vecs))]
    params = pltpu.CompilerParams(dimension_semantics=("arbitrary", "arbitrary"),
                                  vmem_limit_bytes=VMEM_LIMIT_BYTES)

    tm = 256
    assert seq % tm == 0 and tm >= HIST_B
    y_p, tail_a, tail_b = pl.pallas_call(
        functools.partial(_prompt_kernel, tm=tm),
        grid=(n_p, seq // tm),
        in_specs=[pl.BlockSpec((None, tm, D), lambda b, t: (b, t, 0)),
                  pl.BlockSpec((None, tm, PLE), lambda b, t: (b, t, 0))] + const_specs,
        out_specs=[pl.BlockSpec((None, tm, D), lambda b, t: (b, t, 0)),
                   pl.BlockSpec((None, HIST_A, D), lambda b, t: (b, 0, 0)),
                   pl.BlockSpec((None, HIST_B, D), lambda b, t: (b, 0, 0))],
        out_shape=[jax.ShapeDtypeStruct((n_p, seq, D), f32),
                   jax.ShapeDtypeStruct((n_p, HIST_A, D), f32),
                   jax.ShapeDtypeStruct((n_p, HIST_B, D), f32)],
        scratch_shapes=[pltpu.VMEM((N_SLAB, HIST_A + tm, LANES), f32),
                        pltpu.VMEM((N_SLAB, HIST_B + tm, LANES), f32),
                        pltpu.VMEM((tm, D), f32),
                        pltpu.VMEM((tm, D), f32)],
        compiler_params=params,
        name="prompt_layer",
    )(x_prompt, p_prompt[0], *weights, *vecs)
    new_a_p = tail_a[:, HIST_A - (K_A - 1):, :][None]
    new_b_p = tail_b[:, HIST_B - (K_B - 1):, :][None]

    nseq = 32
    assert n_s % nseq == 0 and t_s % SUBLANES == 0
    rows = nseq * t_s
    st_a = jnp.pad(state_conv_a[0], ((0, 0), (HIST_A - (K_A - 1), 0), (0, 0)))
    st_b = jnp.pad(state_conv_b[0], ((0, 0), (HIST_B - (K_B - 1), 0), (0, 0)))
    x_s = x_sample.reshape(n_s * t_s, D)
    p_s = p_sample[0].reshape(n_s * t_s, PLE)
    y_s, s_s, v_s = pl.pallas_call(
        functools.partial(_sample_kernel, nseq=nseq, t_len=t_s),
        grid=(n_s // nseq,),
        in_specs=[pl.BlockSpec((rows, D), lambda i: (i, 0)),
                  pl.BlockSpec((rows, PLE), lambda i: (i, 0)),
                  pl.BlockSpec((nseq, HIST_A, D), lambda i: (i, 0, 0)),
                  pl.BlockSpec((nseq, HIST_B, D), lambda i: (i, 0, 0))] + const_specs,
        out_specs=[pl.BlockSpec((rows, D), lambda i: (i, 0)),
                   pl.BlockSpec((rows, D), lambda i: (i, 0)),
                   pl.BlockSpec((rows, D), lambda i: (i, 0))],
        out_shape=[jax.ShapeDtypeStruct((n_s * t_s, D), f32)] * 3,
        scratch_shapes=[pltpu.VMEM((N_SLAB, nseq * (HIST_A + t_s), LANES), f32),
                        pltpu.VMEM((N_SLAB, nseq * (HIST_B + t_s), LANES), f32),
                        pltpu.VMEM((rows, D), f32),
                        pltpu.VMEM((rows, D), f32)],
        compiler_params=pltpu.CompilerParams(dimension_semantics=("arbitrary",),
                                             vmem_limit_bytes=VMEM_LIMIT_BYTES),
        name="sample_layer",
    )(x_s, p_s, st_a, st_b, *weights, *vecs)
    y_s = y_s.reshape(n_s, t_s, D)
    s_s = s_s.reshape(n_s, t_s, D)
    v_s = v_s.reshape(n_s, t_s, D)
    new_a_s = jnp.concatenate([state_conv_a[0], s_s], axis=1)[:, -(K_A - 1):][None]
    new_b_s = jnp.concatenate([state_conv_b[0], v_s], axis=1)[:, -(K_B - 1):][None]
    return y_p, y_s, new_a_p, new_b_p, new_a_s, new_b_s
```

```python
import functools

import jax
import jax.numpy as jnp
from jax import lax
from jax.experimental import pallas as pl
from jax.experimental.pallas import tpu as pltpu

D = 1024
N_PROJ = 9
PLE = 256
K_A = 3
K_B = 31
EPS = 1e-6
LN_EPS = 1e-5
LANES = 128
SUBLANES = 8
N_SLAB = D // LANES
HIST_A = SUBLANES
HIST_B = -(-(K_B - 1) // SUBLANES) * SUBLANES
CONV_ROWS = 64
VMEM_LIMIT_BYTES = 60 * 1024 * 1024

P_AB, P_AC, P_AH, P_AZ, P_BV, P_BG, P_BZ, P_GA, P_GB = range(N_PROJ)

F32 = jnp.float32
BF16 = jnp.bfloat16


def _sigmoid(x):
    return 0.5 * jnp.tanh(0.5 * x) + 0.5


def _silu(x):
    h = 0.5 * x
    return h * jnp.tanh(h) + h


def _rmsnorm(x, g):
    ms = jnp.mean(x * x, axis=-1, keepdims=True)
    return x * lax.rsqrt(ms + EPS) * g


def _layernorm(x, g, b):
    mu = jnp.mean(x, axis=-1, keepdims=True)
    xc = x - mu
    var = jnp.mean(xc * xc, axis=-1, keepdims=True)
    return xc * lax.rsqrt(var + LN_EPS) * g + b


def _to_slabs(slab_ref, val, nseq, t_len, hist):
    pitch = hist + t_len
    for q in range(nseq):
        for c in range(N_SLAB):
            slab_ref[c, q * pitch + hist:(q + 1) * pitch, :] = (
                val[q * t_len:(q + 1) * t_len, c * LANES:(c + 1) * LANES])


def _dwconv(slab_ref, w_ref, out_ref, nseq, t_len, hist, k_taps):
    pitch = hist + t_len
    first = hist - (k_taps - 1)
    rb = min(CONV_ROWS, t_len)
    for q in range(nseq):
        for c in range(N_SLAB):
            lanes = slice(c * LANES, (c + 1) * LANES)
            for r0 in range(0, t_len, rb):
                acc = None
                for k in range(k_taps):
                    start = q * pitch + first + r0 + k
                    term = slab_ref[c, start:start + rb, :] * w_ref[k:k + 1, lanes]
                    acc = term if acc is None else acc + term
                out_ref[q * t_len + r0:q * t_len + r0 + rb, lanes] = acc


def _pipelined_step(xf_ref, xb_ref, pb_ref, w_in, w_out_a, w_out_b, w_o, w_pe, w_pg, w_conv_a,
                    w_conv_b, g_mix, b_conv_b, ln_g, ln_b, g_ple, g_final, y_ref, scratch,
                    set_hist_a, set_hist_b, nseq, t_len):
    (u_ref, slab_a, slab_b, ca_buf, cb_buf, sbz_ref, gaya_ref, sgb_ref, ybpre_ref) = scratch

    @pl.when(pl.program_id(0) == 0)
    def _():
        for ref in (slab_a, slab_b, cb_buf, sbz_ref, gaya_ref, sgb_ref):
            ref[...] = jnp.zeros(ref.shape, F32)

    def dot(a, w):
        return jnp.dot(a, w, preferred_element_type=F32)

    def proj(j):
        return dot(u_ref[...], w_in[:, j * D:(j + 1) * D])

    u_ref[...] = _rmsnorm(xf_ref[...], g_mix[...]).astype(BF16)

    cb = cb_buf[...] + b_conv_b[...]
    yb_pre = _silu(_layernorm(cb, ln_g[...], ln_b[...])) * sbz_ref[...]
    ybpre_ref[...] = yb_pre.astype(BF16)

    set_hist_b()
    _to_slabs(slab_b, proj(P_BV) * _sigmoid(proj(P_BG)), nseq, t_len, HIST_B)
    set_hist_a()
    _to_slabs(slab_a, proj(P_AC) * proj(P_AH), nseq, t_len, HIST_A)
    _dwconv(slab_a, w_conv_a, ca_buf, nseq, t_len, HIST_A, K_A)
    ya_pre = proj(P_AB) * ca_buf[...] * _silu(proj(P_AZ))
    ya = dot(ya_pre.astype(BF16), w_out_a[...])

    yb = dot(ybpre_ref[...], w_out_b[...])
    m = gaya_ref[...] + sgb_ref[...] * yb
    h = xb_ref[...] + dot(m.astype(BF16), w_o[...])
    pe = dot(pb_ref[...].astype(BF16), w_pe[...])
    r = _rmsnorm(h, g_ple[...]).astype(BF16)
    pg = _sigmoid(dot(r, w_pg[...]))
    h = h + pg * pe
    y_ref[...] = _rmsnorm(h, g_final[...])

    _dwconv(slab_b, w_conv_b, cb_buf, nseq, t_len, HIST_B, K_B)
    sbz_ref[...] = _silu(proj(P_BZ))
    gaya_ref[...] = _sigmoid(proj(P_GA)) * ya
    sgb_ref[...] = _sigmoid(proj(P_GB))


def _scratch_shapes(nseq, t_len):
    tm = nseq * t_len
    return [pltpu.VMEM((tm, D), BF16),
            pltpu.VMEM((N_SLAB, nseq * (HIST_A + t_len), LANES), F32),
            pltpu.VMEM((N_SLAB, nseq * (HIST_B + t_len), LANES), F32),
            pltpu.VMEM((tm, D), F32),
            pltpu.VMEM((tm, D), F32),
            pltpu.VMEM((tm, D), F32),
            pltpu.VMEM((tm, D), F32),
            pltpu.VMEM((tm, D), F32),
            pltpu.VMEM((tm, D), BF16)]


def _prompt_kernel(xf_ref, xb_ref, pb_ref, w_in, w_out_a, w_out_b, w_o, w_pe, w_pg, w_conv_a,
                   w_conv_b, g_mix, b_conv_b, ln_g, ln_b, g_ple, g_final,
                   y_ref, tail_a_ref, tail_b_ref, *scratch, tm, tiles_per_seq, n_tiles):
    slab_a, slab_b = scratch[1], scratch[2]
    tile = jnp.minimum(pl.program_id(0), n_tiles - 1)
    t = tile % tiles_per_seq

    def set_hist_a():
        prev = slab_a[:, tm:tm + HIST_A, :]
        slab_a[:, 0:HIST_A, :] = jnp.where(t == 0, 0.0, prev)

    def set_hist_b():
        prev = slab_b[:, tm:tm + HIST_B, :]
        slab_b[:, 0:HIST_B, :] = jnp.where(t == 0, 0.0, prev)

    _pipelined_step(xf_ref, xb_ref, pb_ref, w_in, w_out_a, w_out_b, w_o, w_pe, w_pg, w_conv_a,
                    w_conv_b, g_mix, b_conv_b, ln_g, ln_b, g_ple, g_final, y_ref, scratch,
                    set_hist_a, set_hist_b, 1, tm)

    @pl.when(t == tiles_per_seq - 1)
    def _():
        for c in range(N_SLAB):
            lanes = slice(c * LANES, (c + 1) * LANES)
            tail_a_ref[:, lanes] = slab_a[c, tm:tm + HIST_A, :]
            tail_b_ref[:, lanes] = slab_b[c, tm:tm + HIST_B, :]


def _sample_kernel(xf_ref, xb_ref, pb_ref, st_a_ref, st_b_ref, w_in, w_out_a, w_out_b, w_o, w_pe,
                   w_pg, w_conv_a, w_conv_b, g_mix, b_conv_b, ln_g, ln_b, g_ple, g_final,
                   y_ref, s_ref, v_ref, *scratch, nseq, t_len):
    slab_a, slab_b = scratch[1], scratch[2]
    pitch_a = HIST_A + t_len
    pitch_b = HIST_B + t_len

    def set_hist_a():
        for q in range(nseq):
            for c in range(N_SLAB):
                lanes = slice(c * LANES, (c + 1) * LANES)
                slab_a[c, q * pitch_a:q * pitch_a + HIST_A, :] = st_a_ref[q, :, lanes]

    def set_hist_b():
        for q in range(nseq):
            for c in range(N_SLAB):
                lanes = slice(c * LANES, (c + 1) * LANES)
                slab_b[c, q * pitch_b:q * pitch_b + HIST_B, :] = st_b_ref[q, :, lanes]

    _pipelined_step(xf_ref, xb_ref, pb_ref, w_in, w_out_a, w_out_b, w_o, w_pe, w_pg, w_conv_a,
                    w_conv_b, g_mix, b_conv_b, ln_g, ln_b, g_ple, g_final, y_ref, scratch,
                    set_hist_a, set_hist_b, nseq, t_len)

    for q in range(nseq):
        for c in range(N_SLAB):
            lanes = slice(c * LANES, (c + 1) * LANES)
            rows = slice(q * t_len, (q + 1) * t_len)
            s_ref[rows, lanes] = slab_a[c, q * pitch_a + HIST_A:(q + 1) * pitch_a, :]
            v_ref[rows, lanes] = slab_b[c, q * pitch_b + HIST_B:(q + 1) * pitch_b, :]


def _whole(memory_space=pltpu.VMEM):
    return pl.BlockSpec(memory_space=memory_space)


def kernel(x_prompt, x_sample, state_conv_a, state_conv_b, p_prompt, p_sample, g_mix, w_in, w_conv_a, w_out_a, w_conv_b, b_conv_b, ln_g, ln_b, w_out_b, w_o, w_pe, g_ple, w_pg, g_final):
    depth = w_in.shape[0]
    assert depth == 1, "single-layer trunk"
    n_p, seq, d = x_prompt.shape
    n_s, t_s, _ = x_sample.shape
    assert d == D and w_in.shape[2] == N_PROJ * D

    weights = (w_in[0].astype(BF16), w_out_a[0].astype(BF16), w_out_b[0].astype(BF16),
               w_o[0].astype(BF16), w_pe[0].astype(BF16), w_pg[0].astype(BF16),
               w_conv_a[0], w_conv_b[0])
    vecs = (g_mix[0][None], b_conv_b[0][None], ln_g[0][None], ln_b[0][None],
            g_ple[0][None], g_final[None])
    const_specs = [_whole() for _ in range(len(weights) + len(vecs))]
    params = pltpu.CompilerParams(dimension_semantics=("arbitrary",),
                                  vmem_limit_bytes=VMEM_LIMIT_BYTES)

    def back(i):
        return (jnp.maximum(i - 1, 0), 0)

    tm = 256
    assert seq % tm == 0 and tm >= HIST_B
    tiles_per_seq = seq // tm
    n_tiles = n_p * tiles_per_seq

    def front(i):
        return (jnp.minimum(i, n_tiles - 1), 0)

    def seq_of_front(i):
        return (jnp.minimum(i, n_tiles - 1) // tiles_per_seq, 0, 0)

    x_flat = x_prompt.reshape(n_p * seq, D)
    y_p, tail_a, tail_b = pl.pallas_call(
        functools.partial(_prompt_kernel, tm=tm, tiles_per_seq=tiles_per_seq, n_tiles=n_tiles),
        grid=(n_tiles + 1,),
        in_specs=[pl.BlockSpec((tm, D), front),
                  pl.BlockSpec((tm, D), back),
                  pl.BlockSpec((tm, PLE), back)] + const_specs,
        out_specs=[pl.BlockSpec((tm, D), back),
                   pl.BlockSpec((None, HIST_A, D), seq_of_front),
                   pl.BlockSpec((None, HIST_B, D), seq_of_front)],
        out_shape=[jax.ShapeDtypeStruct((n_p * seq, D), F32),
                   jax.ShapeDtypeStruct((n_p, HIST_A, D), F32),
                   jax.ShapeDtypeStruct((n_p, HIST_B, D), F32)],
        scratch_shapes=_scratch_shapes(1, tm),
        compiler_params=params,
        name="prompt_layer",
    )(x_flat, x_flat, p_prompt[0].reshape(n_p * seq, PLE), *weights, *vecs)
    y_p = y_p.reshape(n_p, seq, D)
    new_a_p = tail_a[:, HIST_A - (K_A - 1):, :][None]
    new_b_p = tail_b[:, HIST_B - (K_B - 1):, :][None]

    nseq = 16
    assert n_s % nseq == 0 and t_s % SUBLANES == 0
    rows = nseq * t_s
    n_tiles_s = n_s // nseq

    def front_s(i):
        return (jnp.minimum(i, n_tiles_s - 1), 0)

    def front_s3(i):
        return (jnp.minimum(i, n_tiles_s - 1), 0, 0)

    st_a = jnp.pad(state_conv_a[0], ((0, 0), (HIST_A - (K_A - 1), 0), (0, 0)))
    st_b = jnp.pad(state_conv_b[0], ((0, 0), (HIST_B - (K_B - 1), 0), (0, 0)))
    x_s = x_sample.reshape(n_s * t_s, D)
    p_s = p_sample[0].reshape(n_s * t_s, PLE)
    y_s, s_s, v_s = pl.pallas_call(
        functools.partial(_sample_kernel, nseq=nseq, t_len=t_s),
        grid=(n_tiles_s + 1,),
        in_specs=[pl.BlockSpec((rows, D), front_s),
                  pl.BlockSpec((rows, D), back),
                  pl.BlockSpec((rows, PLE), back),
                  pl.BlockSpec((nseq, HIST_A, D), front_s3),
                  pl.BlockSpec((nseq, HIST_B, D), front_s3)] + const_specs,
        out_specs=[pl.BlockSpec((rows, D), back),
                   pl.BlockSpec((rows, D), front_s),
                   pl.BlockSpec((rows, D), front_s)],
        out_shape=[jax.ShapeDtypeStruct((n_s * t_s, D), F32)] * 3,
        scratch_shapes=_scratch_shapes(nseq, t_s),
        compiler_params=params,
        name="sample_layer",
    )(x_s, x_s, p_s, st_a, st_b, *weights, *vecs)
    y_s = y_s.reshape(n_s, t_s, D)
    s_s = s_s.reshape(n_s, t_s, D)
    v_s = v_s.reshape(n_s, t_s, D)
    new_a_s = jnp.concatenate([state_conv_a[0], s_s], axis=1)[:, -(K_A - 1):][None]
    new_b_s = jnp.concatenate([state_conv_b[0], v_s], axis=1)[:, -(K_B - 1):][None]
    return y_p, y_s, new_a_p, new_b_p, new_a_s, new_b_s
```

```python
import functools

import jax
import jax.numpy as jnp
from jax import lax
from jax.experimental import pallas as pl
from jax.experimental.pallas import tpu as pltpu

D = 1024
N_PROJ = 9
PLE = 256
K_A = 3
K_B = 31
EPS = 1e-6
LN_EPS = 1e-5
LANES = 128
SUBLANES = 8
N_SLAB = D // LANES
HIST_A = SUBLANES
HIST_B = -(-(K_B - 1) // SUBLANES) * SUBLANES
CONV_ROWS = 64
VMEM_LIMIT_BYTES = 60 * 1024 * 1024

P_AB, P_AC, P_AH, P_AZ, P_BV, P_BG, P_BZ, P_GA, P_GB = range(N_PROJ)

F32 = jnp.float32
BF16 = jnp.bfloat16


def _sigmoid(x):
    return 0.5 * jnp.tanh(0.5 * x) + 0.5


def _silu(x):
    h = 0.5 * x
    return h * jnp.tanh(h) + h


def _rmsnorm(x, g):
    ms = jnp.mean(x * x, axis=-1, keepdims=True)
    return x * lax.rsqrt(ms + EPS) * g


def _layernorm(x, g, b):
    mu = jnp.mean(x, axis=-1, keepdims=True)
    xc = x - mu
    var = jnp.mean(xc * xc, axis=-1, keepdims=True)
    return xc * lax.rsqrt(var + LN_EPS) * g + b


def _lanes(c):
    return slice(c * LANES, (c + 1) * LANES)


def _pipelined_step(load_xf, load_xb, load_pb, store_y, weights, vecs, bufs, convs):
    w_in, w_out_a, w_out_b, w_o, w_pe, w_pg = weights
    g_mix, b_conv_b, ln_g, ln_b, g_ple, g_final = vecs
    u_ref, ca_buf, cb_buf, sbz_ref, gaya_ref, sgb_ref, ybpre_ref = bufs

    def dot(a, w):
        return jnp.dot(a, w, preferred_element_type=F32)

    def proj(j):
        return dot(u_ref[...], w_in[:, j * D:(j + 1) * D])

    u_ref[...] = _rmsnorm(load_xf(), g_mix[...]).astype(BF16)

    cb = cb_buf[...] + b_conv_b[...]
    yb_pre = _silu(_layernorm(cb, ln_g[...], ln_b[...])) * sbz_ref[...]
    ybpre_ref[...] = yb_pre.astype(BF16)

    convs.put_v(proj(P_BV) * _sigmoid(proj(P_BG)))
    convs.put_s(proj(P_AC) * proj(P_AH))
    convs.conv_a()
    ya_pre = proj(P_AB) * ca_buf[...] * _silu(proj(P_AZ))
    ya = dot(ya_pre.astype(BF16), w_out_a[...])

    yb = dot(ybpre_ref[...], w_out_b[...])
    m = gaya_ref[...] + sgb_ref[...] * yb
    h = load_xb() + dot(m.astype(BF16), w_o[...])
    pe = dot(load_pb().astype(BF16), w_pe[...])
    r = _rmsnorm(h, g_ple[...]).astype(BF16)
    pg = _sigmoid(dot(r, w_pg[...]))
    h = h + pg * pe
    store_y(_rmsnorm(h, g_final[...]))

    convs.conv_b()
    sbz_ref[...] = _silu(proj(P_BZ))
    gaya_ref[...] = _sigmoid(proj(P_GA)) * ya
    sgb_ref[...] = _sigmoid(proj(P_GB))


def _zero_first_step(refs):
    @pl.when(pl.program_id(0) == 0)
    def _():
        for ref in refs:
            ref[...] = jnp.zeros(ref.shape, ref.dtype)


def _common_scratch(tm):
    return [pltpu.VMEM((tm, D), BF16),
            pltpu.VMEM((tm, D), F32),
            pltpu.VMEM((tm, D), F32),
            pltpu.VMEM((tm, D), F32),
            pltpu.VMEM((tm, D), F32),
            pltpu.VMEM((tm, D), F32),
            pltpu.VMEM((tm, D), BF16)]


N_COMMON = 7
N_WEIGHTS = 6
N_VECS = 6


class _PromptConvs:
    def __init__(self, slab_a, slab_b, ca_buf, cb_buf, w_conv_a, w_conv_b, tm, t):
        self.slab_a, self.slab_b, self.ca_buf, self.cb_buf = slab_a, slab_b, ca_buf, cb_buf
        self.w_conv_a, self.w_conv_b, self.tm, self.t = w_conv_a, w_conv_b, tm, t

    def _put(self, slab, hist, val):
        prev = slab[:, self.tm:self.tm + hist, :]
        slab[:, 0:hist, :] = jnp.where(self.t == 0, 0.0, prev)
        for c in range(N_SLAB):
            slab[c, hist:hist + self.tm, :] = val[:, _lanes(c)]

    def put_v(self, v):
        self._put(self.slab_b, HIST_B, v)

    def put_s(self, s):
        self._put(self.slab_a, HIST_A, s)

    def _conv(self, slab, w_ref, out_ref, hist, k_taps):
        first = hist - (k_taps - 1)
        for c in range(N_SLAB):
            for r0 in range(0, self.tm, CONV_ROWS):
                acc = None
                for k in range(k_taps):
                    start = first + r0 + k
                    term = slab[c, start:start + CONV_ROWS, :] * w_ref[k:k + 1, _lanes(c)]
                    acc = term if acc is None else acc + term
                out_ref[r0:r0 + CONV_ROWS, _lanes(c)] = acc

    def conv_a(self):
        self._conv(self.slab_a, self.w_conv_a, self.ca_buf, HIST_A, K_A)

    def conv_b(self):
        self._conv(self.slab_b, self.w_conv_b, self.cb_buf, HIST_B, K_B)


def _prompt_kernel(*refs, tm, tiles_per_seq, n_tiles):
    xf_ref, xb_ref, pb_ref = refs[:3]
    weights = refs[3:3 + N_WEIGHTS]
    w_conv_a, w_conv_b = refs[9:11]
    vecs = refs[11:11 + N_VECS]
    y_ref, tail_a_ref, tail_b_ref = refs[17:20]
    bufs = refs[20:20 + N_COMMON]
    slab_a, slab_b = refs[20 + N_COMMON:]
    ca_buf, cb_buf, sbz_ref, gaya_ref, sgb_ref = bufs[1:6]

    tile = jnp.minimum(pl.program_id(0), n_tiles - 1)
    t = tile % tiles_per_seq
    _zero_first_step((slab_a, slab_b, cb_buf, sbz_ref, gaya_ref, sgb_ref))

    def store_y(val):
        y_ref[...] = val

    convs = _PromptConvs(slab_a, slab_b, ca_buf, cb_buf, w_conv_a, w_conv_b, tm, t)
    _pipelined_step(lambda: xf_ref[...], lambda: xb_ref[...], lambda: pb_ref[...], store_y,
                    weights, vecs, bufs, convs)

    @pl.when(t == tiles_per_seq - 1)
    def _():
        for c in range(N_SLAB):
            tail_a_ref[:, _lanes(c)] = slab_a[c, tm:tm + HIST_A, :]
            tail_b_ref[:, _lanes(c)] = slab_b[c, tm:tm + HIST_B, :]


class _SampleConvs:
    def __init__(self, st_a_ref, st_b_ref, s_ref, v_ref, ca_buf, cb_buf, w_conv_a, w_conv_b,
                 ns, t_len):
        self.st_a_ref, self.st_b_ref, self.s_ref, self.v_ref = st_a_ref, st_b_ref, s_ref, v_ref
        self.ca_buf, self.cb_buf = ca_buf, cb_buf
        self.w_conv_a, self.w_conv_b, self.ns, self.t_len = w_conv_a, w_conv_b, ns, t_len

    def put_v(self, v):
        self.v_ref[...] = v

    def put_s(self, s):
        self.s_ref[...] = s

    def _conv(self, st_ref, new_ref, w_ref, out_ref, k_taps):
        ns, t_len, hist = self.ns, self.t_len, k_taps - 1
        for c in range(N_SLAB):
            w = [w_ref[k:k + 1, _lanes(c)] for k in range(k_taps)]
            for n0 in range(0, ns, SUBLANES):
                acc = [None] * t_len
                for e in range(hist + t_len):
                    if e < hist:
                        x = st_ref[e, n0:n0 + SUBLANES, _lanes(c)]
                    else:
                        r = (e - hist) * ns + n0
                        x = new_ref[r:r + SUBLANES, _lanes(c)]
                    for t in range(t_len):
                        k = e - t
                        if 0 <= k < k_taps:
                            term = x * w[k]
                            acc[t] = term if acc[t] is None else acc[t] + term
                for t in range(t_len):
                    out_ref[t * ns + n0:t * ns + n0 + SUBLANES, _lanes(c)] = acc[t]

    def conv_a(self):
        self._conv(self.st_a_ref, self.s_ref, self.w_conv_a, self.ca_buf, K_A)

    def conv_b(self):
        self._conv(self.st_b_ref, self.v_ref, self.w_conv_b, self.cb_buf, K_B)


def _sample_kernel(*refs, ns, t_len):
    xf_ref, xb_ref, pb_ref, st_a_ref, st_b_ref = refs[:5]
    weights = refs[5:5 + N_WEIGHTS]
    w_conv_a, w_conv_b = refs[11:13]
    vecs = refs[13:13 + N_VECS]
    y_ref, new_a_ref, new_b_ref = refs[19:22]
    bufs = refs[22:22 + N_COMMON]
    s_ref, v_ref = refs[22 + N_COMMON:]
    ca_buf, cb_buf, sbz_ref, gaya_ref, sgb_ref = bufs[1:6]
    rows = ns * t_len

    _zero_first_step((cb_buf, sbz_ref, gaya_ref, sgb_ref))

    def store_y(val):
        y_ref[...] = val.reshape(t_len, ns, D)

    convs = _SampleConvs(st_a_ref, st_b_ref, s_ref, v_ref, ca_buf, cb_buf, w_conv_a, w_conv_b,
                         ns, t_len)
    _pipelined_step(lambda: xf_ref[...].reshape(rows, D), lambda: xb_ref[...].reshape(rows, D),
                    lambda: pb_ref[...].reshape(rows, PLE), store_y, weights, vecs, bufs, convs)

    for e in range(K_A - 1):
        src = e + t_len - (K_A - 1)
        new_a_ref[e] = s_ref[src * ns:(src + 1) * ns, :]
    for e in range(K_B - 1):
        src = e + t_len
        if src < K_B - 1:
            new_b_ref[e] = st_b_ref[src]
        else:
            src -= K_B - 1
            new_b_ref[e] = v_ref[src * ns:(src + 1) * ns, :]


def _whole(memory_space=pltpu.VMEM):
    return pl.BlockSpec(memory_space=memory_space)


def kernel(x_prompt, x_sample, state_conv_a, state_conv_b, p_prompt, p_sample, g_mix, w_in, w_conv_a, w_out_a, w_conv_b, b_conv_b, ln_g, ln_b, w_out_b, w_o, w_pe, g_ple, w_pg, g_final):
    depth = w_in.shape[0]
    assert depth == 1, "single-layer trunk"
    n_p, seq, d = x_prompt.shape
    n_s, t_s, _ = x_sample.shape
    assert d == D and w_in.shape[2] == N_PROJ * D
    assert t_s >= K_A - 1 and t_s <= K_B - 1

    consts = (w_in[0].astype(BF16), w_out_a[0].astype(BF16), w_out_b[0].astype(BF16),
              w_o[0].astype(BF16), w_pe[0].astype(BF16), w_pg[0].astype(BF16),
              w_conv_a[0], w_conv_b[0],
              g_mix[0][None], b_conv_b[0][None], ln_g[0][None], ln_b[0][None],
              g_ple[0][None], g_final[None])
    const_specs = [_whole() for _ in consts]
    params = pltpu.CompilerParams(dimension_semantics=("arbitrary",),
                                  vmem_limit_bytes=VMEM_LIMIT_BYTES)

    tm = 256
    assert seq % tm == 0 and tm >= HIST_B and tm % CONV_ROWS == 0
    tiles_per_seq = seq // tm
    n_tiles = n_p * tiles_per_seq

    def front(i):
        return (jnp.minimum(i, n_tiles - 1), 0)

    def back(i):
        return (jnp.maximum(i - 1, 0), 0)

    def seq_of_front(i):
        return (jnp.minimum(i, n_tiles - 1) // tiles_per_seq, 0, 0)

    x_flat = x_prompt.reshape(n_p * seq, D)
    y_p, tail_a, tail_b = pl.pallas_call(
        functools.partial(_prompt_kernel, tm=tm, tiles_per_seq=tiles_per_seq, n_tiles=n_tiles),
        grid=(n_tiles + 1,),
        in_specs=[pl.BlockSpec((tm, D), front),
                  pl.BlockSpec((tm, D), back),
                  pl.BlockSpec((tm, PLE), back)] + const_specs,
        out_specs=[pl.BlockSpec((tm, D), back),
                   pl.BlockSpec((None, HIST_A, D), seq_of_front),
                   pl.BlockSpec((None, HIST_B, D), seq_of_front)],
        out_shape=[jax.ShapeDtypeStruct((n_p * seq, D), F32),
                   jax.ShapeDtypeStruct((n_p, HIST_A, D), F32),
                   jax.ShapeDtypeStruct((n_p, HIST_B, D), F32)],
        scratch_shapes=_common_scratch(tm) + [
            pltpu.VMEM((N_SLAB, HIST_A + tm, LANES), F32),
            pltpu.VMEM((N_SLAB, HIST_B + tm, LANES), F32)],
        compiler_params=params,
        name="prompt_layer",
    )(x_flat, x_flat, p_prompt[0].reshape(n_p * seq, PLE), *consts)
    y_p = y_p.reshape(n_p, seq, D)
    new_a_p = tail_a[:, HIST_A - (K_A - 1):, :][None]
    new_b_p = tail_b[:, HIST_B - (K_B - 1):, :][None]

    ns = 16
    assert n_s % ns == 0 and ns % SUBLANES == 0
    n_tiles_s = n_s // ns

    def front_s(i):
        return (0, jnp.minimum(i, n_tiles_s - 1), 0)

    def back_s(i):
        return (0, jnp.maximum(i - 1, 0), 0)

    def t_major(a):
        return jnp.transpose(a, (1, 0, 2))

    x_t = t_major(x_sample)
    y_t, new_a_t, new_b_t = pl.pallas_call(
        functools.partial(_sample_kernel, ns=ns, t_len=t_s),
        grid=(n_tiles_s + 1,),
        in_specs=[pl.BlockSpec((t_s, ns, D), front_s),
                  pl.BlockSpec((t_s, ns, D), back_s),
                  pl.BlockSpec((t_s, ns, PLE), back_s),
                  pl.BlockSpec((K_A - 1, ns, D), front_s),
                  pl.BlockSpec((K_B - 1, ns, D), front_s)] + const_specs,
        out_specs=[pl.BlockSpec((t_s, ns, D), back_s),
                   pl.BlockSpec((K_A - 1, ns, D), front_s),
                   pl.BlockSpec((K_B - 1, ns, D), front_s)],
        out_shape=[jax.ShapeDtypeStruct((t_s, n_s, D), F32),
                   jax.ShapeDtypeStruct((K_A - 1, n_s, D), F32),
                   jax.ShapeDtypeStruct((K_B - 1, n_s, D), F32)],
        scratch_shapes=_common_scratch(ns * t_s) + [
            pltpu.VMEM((ns * t_s, D), F32),
            pltpu.VMEM((ns * t_s, D), F32)],
        compiler_params=params,
        name="sample_layer",
    )(x_t, x_t, t_major(p_sample[0]), t_major(state_conv_a[0]), t_major(state_conv_b[0]),
      *consts)
    return (y_p, t_major(y_t), new_a_p, new_b_p, t_major(new_a_t)[None], t_major(new_b_t)[None])
```

```python
import functools

import jax
import jax.numpy as jnp
from jax import lax
from jax.experimental import pallas as pl
from jax.experimental.pallas import tpu as pltpu

D = 1024
N_PROJ = 9
PLE = 256
K_A = 3
K_B = 31
EPS = 1e-6
LN_EPS = 1e-5
LANES = 128
SUBLANES = 8
N_SLAB = D // LANES
HIST_A = SUBLANES
HIST_B = -(-(K_B - 1) // SUBLANES) * SUBLANES
CONV_ROWS = 64
VMEM_LIMIT_BYTES = 60 * 1024 * 1024

P_AB, P_AC, P_AH, P_AZ, P_BV, P_BG, P_BZ, P_GA, P_GB = range(N_PROJ)
PAIRS = ((P_BV, P_BG), (P_AC, P_AH), (P_AB, P_AZ))
PAIR_VG, PAIR_CH, PAIR_BZ = range(len(PAIRS))
RESTS = (P_BZ, P_GA, P_GB)
REST_BZ, REST_GA, REST_GB = range(len(RESTS))

F32 = jnp.float32
BF16 = jnp.bfloat16


def _sigmoid(x):
    return 0.5 * jnp.tanh(0.5 * x) + 0.5


def _silu(x):
    h = 0.5 * x
    return h * jnp.tanh(h) + h


def _rmsnorm(x, g):
    ms = jnp.mean(x * x, axis=-1, keepdims=True)
    return x * lax.rsqrt(ms + EPS) * g


def _layernorm(x, g, b):
    mu = jnp.mean(x, axis=-1, keepdims=True)
    xc = x - mu
    var = jnp.mean(xc * xc, axis=-1, keepdims=True)
    return xc * lax.rsqrt(var + LN_EPS) * g + b


def _lanes(c):
    return slice(c * LANES, (c + 1) * LANES)


def _pipelined_step(load_xf, load_xb, load_pb, store_y, weights, vecs, bufs, convs):
    w_pairs, w_rest, w_out_a, w_out_b, w_o, w_pe, w_pg = weights
    g_mix, b_conv_b, ln_g, ln_b, g_ple, g_final = vecs
    u_ref, ca_buf, cb_buf, sbz_ref, gaya_ref, sgb_ref, ybpre_ref, yapre_ref = bufs

    def dot(a, w):
        return jnp.dot(a, w, preferred_element_type=F32)

    def proj(j):
        return dot(u_ref[...], w_rest[:, j * D:(j + 1) * D])

    def proj_pair(g):
        r = dot(u_ref[...], w_pairs[:, g * 2 * D:(g + 1) * 2 * D])
        return [(r[:, 2 * c * LANES:(2 * c + 1) * LANES],
                 r[:, (2 * c + 1) * LANES:(2 * c + 2) * LANES]) for c in range(N_SLAB)]

    u_ref[...] = _rmsnorm(load_xf(), g_mix[...]).astype(BF16)

    cb = cb_buf[...] + b_conv_b[...]
    yb_pre = _silu(_layernorm(cb, ln_g[...], ln_b[...])) * sbz_ref[...]
    ybpre_ref[...] = yb_pre.astype(BF16)

    convs.put_v([bv * _sigmoid(bg) for bv, bg in proj_pair(PAIR_VG)])
    convs.put_s([ac * ah for ac, ah in proj_pair(PAIR_CH)])
    convs.conv_a()
    for c, (ab, az) in enumerate(proj_pair(PAIR_BZ)):
        yapre_ref[:, _lanes(c)] = (ab * ca_buf[:, _lanes(c)] * _silu(az)).astype(BF16)
    ya = dot(yapre_ref[...], w_out_a[...])

    yb = dot(ybpre_ref[...], w_out_b[...])
    m = gaya_ref[...] + sgb_ref[...] * yb
    h = load_xb() + dot(m.astype(BF16), w_o[...])
    pe = dot(load_pb().astype(BF16), w_pe[...])
    r = _rmsnorm(h, g_ple[...]).astype(BF16)
    pg = _sigmoid(dot(r, w_pg[...]))
    h = h + pg * pe
    store_y(_rmsnorm(h, g_final[...]))

    convs.conv_b(range(0, N_SLAB // 2))
    sbz_ref[...] = _silu(proj(REST_BZ))
    gaya_ref[...] = _sigmoid(proj(REST_GA)) * ya
    sgb_ref[...] = _sigmoid(proj(REST_GB))
    convs.conv_b(range(N_SLAB // 2, N_SLAB))


def _zero_first_step(refs):
    @pl.when(pl.program_id(0) == 0)
    def _():
        for ref in refs:
            ref[...] = jnp.zeros(ref.shape, ref.dtype)


def _common_scratch(tm):
    return [pltpu.VMEM((tm, D), BF16),
            pltpu.VMEM((tm, D), F32),
            pltpu.VMEM((tm, D), F32),
            pltpu.VMEM((tm, D), F32),
            pltpu.VMEM((tm, D), F32),
            pltpu.VMEM((tm, D), F32),
            pltpu.VMEM((tm, D), BF16),
            pltpu.VMEM((tm, D), BF16)]


N_COMMON = 8
N_WEIGHTS = 7
N_VECS = 6


def _split_consts(refs):
    a, b, c = N_WEIGHTS, N_WEIGHTS + 2, N_WEIGHTS + 2 + N_VECS
    return refs[:a], refs[a:b], refs[b:c], refs[c:]


class _PromptConvs:
    def __init__(self, slab_a, slab_b, ca_buf, cb_buf, w_conv_a, w_conv_b, tm, t):
        self.slab_a, self.slab_b, self.ca_buf, self.cb_buf = slab_a, slab_b, ca_buf, cb_buf
        self.w_conv_a, self.w_conv_b, self.tm, self.t = w_conv_a, w_conv_b, tm, t

    def _put(self, slab, hist, pieces):
        prev = slab[:, self.tm:self.tm + hist, :]
        slab[:, 0:hist, :] = jnp.where(self.t == 0, 0.0, prev)
        for c, piece in enumerate(pieces):
            slab[c, hist:hist + self.tm, :] = piece

    def put_v(self, v):
        self._put(self.slab_b, HIST_B, v)

    def put_s(self, s):
        self._put(self.slab_a, HIST_A, s)

    def _conv(self, slab, w_ref, out_ref, hist, k_taps, slabs=range(N_SLAB)):
        first = hist - (k_taps - 1)
        for c in slabs:
            for r0 in range(0, self.tm, CONV_ROWS):
                acc = None
                for k in range(k_taps):
                    start = first + r0 + k
                    term = slab[c, start:start + CONV_ROWS, :] * w_ref[k:k + 1, _lanes(c)]
                    acc = term if acc is None else acc + term
                out_ref[r0:r0 + CONV_ROWS, _lanes(c)] = acc

    def conv_a(self):
        self._conv(self.slab_a, self.w_conv_a, self.ca_buf, HIST_A, K_A)

    def conv_b(self, slabs=range(N_SLAB)):
        self._conv(self.slab_b, self.w_conv_b, self.cb_buf, HIST_B, K_B, slabs)


def _prompt_kernel(*refs, tm, tiles_per_seq, n_tiles):
    n_in = 3
    xf_ref, xb_ref, pb_ref = refs[:n_in]
    weights, (w_conv_a, w_conv_b), vecs, rest = _split_consts(refs[n_in:])
    y_ref, tail_a_ref, tail_b_ref = rest[:3]
    bufs = rest[3:3 + N_COMMON]
    slab_a, slab_b = rest[3 + N_COMMON:]
    ca_buf, cb_buf, sbz_ref, gaya_ref, sgb_ref = bufs[1:6]

    tile = jnp.minimum(pl.program_id(0), n_tiles - 1)
    t = tile % tiles_per_seq
    _zero_first_step((slab_a, slab_b, cb_buf, sbz_ref, gaya_ref, sgb_ref))

    def store_y(val):
        y_ref[...] = val

    convs = _PromptConvs(slab_a, slab_b, ca_buf, cb_buf, w_conv_a, w_conv_b, tm, t)
    _pipelined_step(lambda: xf_ref[...], lambda: xb_ref[...], lambda: pb_ref[...], store_y,
                    weights, vecs, bufs, convs)

    @pl.when(t == tiles_per_seq - 1)
    def _():
        for c in range(N_SLAB):
            tail_a_ref[:, _lanes(c)] = slab_a[c, tm:tm + HIST_A, :]
            tail_b_ref[:, _lanes(c)] = slab_b[c, tm:tm + HIST_B, :]


class _SampleConvs:
    def __init__(self, st_a_ref, st_b_ref, s_ref, v_ref, ca_buf, cb_buf, w_conv_a, w_conv_b,
                 ns, t_len):
        self.st_a_ref, self.st_b_ref, self.s_ref, self.v_ref = st_a_ref, st_b_ref, s_ref, v_ref
        self.ca_buf, self.cb_buf = ca_buf, cb_buf
        self.w_conv_a, self.w_conv_b, self.ns, self.t_len = w_conv_a, w_conv_b, ns, t_len

    def put_v(self, pieces):
        for c, piece in enumerate(pieces):
            self.v_ref[:, _lanes(c)] = piece

    def put_s(self, pieces):
        for c, piece in enumerate(pieces):
            self.s_ref[:, _lanes(c)] = piece

    def _conv(self, st_ref, new_ref, w_ref, out_ref, k_taps, slabs=range(N_SLAB)):
        ns, t_len, hist = self.ns, self.t_len, k_taps - 1
        for c in slabs:
            w = [w_ref[k:k + 1, _lanes(c)] for k in range(k_taps)]
            for n0 in range(0, ns, SUBLANES):
                acc = [None] * t_len
                for e in range(hist + t_len):
                    if e < hist:
                        x = st_ref[e, n0:n0 + SUBLANES, _lanes(c)]
                    else:
                        r = (e - hist) * ns + n0
                        x = new_ref[r:r + SUBLANES, _lanes(c)]
                    for t in range(t_len):
                        k = e - t
                        if 0 <= k < k_taps:
                            term = x * w[k]
                            acc[t] = term if acc[t] is None else acc[t] + term
                for t in range(t_len):
                    out_ref[t * ns + n0:t * ns + n0 + SUBLANES, _lanes(c)] = acc[t]

    def conv_a(self):
        self._conv(self.st_a_ref, self.s_ref, self.w_conv_a, self.ca_buf, K_A)

    def conv_b(self, slabs=range(N_SLAB)):
        self._conv(self.st_b_ref, self.v_ref, self.w_conv_b, self.cb_buf, K_B, slabs)


def _sample_kernel(*refs, ns, t_len):
    n_in = 5
    xf_ref, xb_ref, pb_ref, st_a_ref, st_b_ref = refs[:n_in]
    weights, (w_conv_a, w_conv_b), vecs, rest = _split_consts(refs[n_in:])
    y_ref, new_a_ref, new_b_ref = rest[:3]
    bufs = rest[3:3 + N_COMMON]
    s_ref, v_ref = rest[3 + N_COMMON:]
    ca_buf, cb_buf, sbz_ref, gaya_ref, sgb_ref = bufs[1:6]
    rows = ns * t_len

    _zero_first_step((cb_buf, sbz_ref, gaya_ref, sgb_ref))

    def store_y(val):
        y_ref[...] = val.reshape(t_len, ns, D)

    convs = _SampleConvs(st_a_ref, st_b_ref, s_ref, v_ref, ca_buf, cb_buf, w_conv_a, w_conv_b,
                         ns, t_len)
    _pipelined_step(lambda: xf_ref[...].reshape(rows, D), lambda: xb_ref[...].reshape(rows, D),
                    lambda: pb_ref[...].reshape(rows, PLE), store_y, weights, vecs, bufs, convs)

    for e in range(K_A - 1):
        src = e + t_len - (K_A - 1)
        new_a_ref[e] = s_ref[src * ns:(src + 1) * ns, :]
    for e in range(K_B - 1):
        src = e + t_len
        if src < K_B - 1:
            new_b_ref[e] = st_b_ref[src]
        else:
            src -= K_B - 1
            new_b_ref[e] = v_ref[src * ns:(src + 1) * ns, :]


def _whole(memory_space=pltpu.VMEM):
    return pl.BlockSpec(memory_space=memory_space)


def kernel(x_prompt, x_sample, state_conv_a, state_conv_b, p_prompt, p_sample, g_mix, w_in, w_conv_a, w_out_a, w_conv_b, b_conv_b, ln_g, ln_b, w_out_b, w_o, w_pe, g_ple, w_pg, g_final):
    depth = w_in.shape[0]
    assert depth == 1, "single-layer trunk"
    n_p, seq, d = x_prompt.shape
    n_s, t_s, _ = x_sample.shape
    assert d == D and w_in.shape[2] == N_PROJ * D
    assert t_s >= K_A - 1 and t_s <= K_B - 1

    w_blocks = w_in[0].reshape(D, N_PROJ, N_SLAB, LANES)
    w_pairs = jnp.concatenate(
        [jnp.stack([w_blocks[:, j1], w_blocks[:, j2]], axis=2).reshape(D, 2 * D)
         for j1, j2 in PAIRS], axis=1)
    w_rest = jnp.concatenate([w_blocks[:, j].reshape(D, D) for j in RESTS], axis=1)
    consts = (w_pairs.astype(BF16), w_rest.astype(BF16),
              w_out_a[0].astype(BF16), w_out_b[0].astype(BF16),
              w_o[0].astype(BF16), w_pe[0].astype(BF16), w_pg[0].astype(BF16),
              w_conv_a[0], w_conv_b[0],
              g_mix[0][None], b_conv_b[0][None], ln_g[0][None], ln_b[0][None],
              g_ple[0][None], g_final[None])
    const_specs = [_whole() for _ in consts]
    params = pltpu.CompilerParams(dimension_semantics=("arbitrary",),
                                  vmem_limit_bytes=VMEM_LIMIT_BYTES)

    tm = 256
    assert seq % tm == 0 and tm >= HIST_B and tm % CONV_ROWS == 0
    tiles_per_seq = seq // tm
    n_tiles = n_p * tiles_per_seq

    def front(i):
        return (jnp.minimum(i, n_tiles - 1), 0)

    def back(i):
        return (jnp.maximum(i - 1, 0), 0)

    def seq_of_front(i):
        return (jnp.minimum(i, n_tiles - 1) // tiles_per_seq, 0, 0)

    x_flat = x_prompt.reshape(n_p * seq, D)
    y_p, tail_a, tail_b = pl.pallas_call(
        functools.partial(_prompt_kernel, tm=tm, tiles_per_seq=tiles_per_seq, n_tiles=n_tiles),
        grid=(n_tiles + 1,),
        in_specs=[pl.BlockSpec((tm, D), front),
                  pl.BlockSpec((tm, D), back),
                  pl.BlockSpec((tm, PLE), back)] + const_specs,
        out_specs=[pl.BlockSpec((tm, D), back),
                   pl.BlockSpec((None, HIST_A, D), seq_of_front),
                   pl.BlockSpec((None, HIST_B, D), seq_of_front)],
        out_shape=[jax.ShapeDtypeStruct((n_p * seq, D), F32),
                   jax.ShapeDtypeStruct((n_p, HIST_A, D), F32),
                   jax.ShapeDtypeStruct((n_p, HIST_B, D), F32)],
        scratch_shapes=_common_scratch(tm) + [
            pltpu.VMEM((N_SLAB, HIST_A + tm, LANES), F32),
            pltpu.VMEM((N_SLAB, HIST_B + tm, LANES), F32)],
        compiler_params=params,
        name="prompt_layer",
    )(x_flat, x_flat, p_prompt[0].reshape(n_p * seq, PLE), *consts)
    y_p = y_p.reshape(n_p, seq, D)
    new_a_p = tail_a[:, HIST_A - (K_A - 1):, :][None]
    new_b_p = tail_b[:, HIST_B - (K_B - 1):, :][None]

    ns = 16
    assert n_s % ns == 0 and ns % SUBLANES == 0
    n_tiles_s = n_s // ns

    def front_s(i):
        return (0, jnp.minimum(i, n_tiles_s - 1), 0)

    def back_s(i):
        return (0, jnp.maximum(i - 1, 0), 0)

    def t_major(a):
        return jnp.transpose(a, (1, 0, 2))

    x_t = t_major(x_sample)
    y_t, new_a_t, new_b_t = pl.pallas_call(
        functools.partial(_sample_kernel, ns=ns, t_len=t_s),
        grid=(n_tiles_s + 1,),
        in_specs=[pl.BlockSpec((t_s, ns, D), front_s),
                  pl.BlockSpec((t_s, ns, D), back_s),
                  pl.BlockSpec((t_s, ns, PLE), back_s),
                  pl.BlockSpec((K_A - 1, ns, D), front_s),
                  pl.BlockSpec((K_B - 1, ns, D), front_s)] + const_specs,
        out_specs=[pl.BlockSpec((t_s, ns, D), back_s),
                   pl.BlockSpec((K_A - 1, ns, D), front_s),
                   pl.BlockSpec((K_B - 1, ns, D), front_s)],
        out_shape=[jax.ShapeDtypeStruct((t_s, n_s, D), F32),
                   jax.ShapeDtypeStruct((K_A - 1, n_s, D), F32),
                   jax.ShapeDtypeStruct((K_B - 1, n_s, D), F32)],
        scratch_shapes=_common_scratch(ns * t_s) + [
            pltpu.VMEM((ns * t_s, D), F32),
            pltpu.VMEM((ns * t_s, D), F32)],
        compiler_params=params,
        name="sample_layer",
    )(x_t, x_t, t_major(p_sample[0]), t_major(state_conv_a[0]), t_major(state_conv_b[0]),
      *consts)
    return (y_p, t_major(y_t), new_a_p, new_b_p, t_major(new_a_t)[None], t_major(new_b_t)[None])
```

```python
import functools

import jax
import jax.numpy as jnp
from jax import lax
from jax.experimental import pallas as pl
from jax.experimental.pallas import tpu as pltpu

D = 1024
N_PROJ = 9
PLE = 256
K_A = 3
K_B = 31
EPS = 1e-6
LN_EPS = 1e-5
LANES = 128
SUBLANES = 8
N_SLAB = D // LANES
HIST_A = SUBLANES
HIST_B = -(-(K_B - 1) // SUBLANES) * SUBLANES
CONV_ROWS = 64
VMEM_LIMIT_BYTES = 60 * 1024 * 1024

P_AB, P_AC, P_AH, P_AZ, P_BV, P_BG, P_BZ, P_GA, P_GB = range(N_PROJ)
PAIRS = ((P_BV, P_BG), (P_AC, P_AH), (P_AB, P_AZ))
SINGLES = (P_BZ, P_GA, P_GB)
W_IN_ORDER = (tuple((j, c) for pair in PAIRS for c in range(D // 128) for j in pair)
              + tuple((j, c) for j in SINGLES for c in range(D // 128)))

F32 = jnp.float32
BF16 = jnp.bfloat16


def _sigmoid(x):
    return 0.5 * jnp.tanh(0.5 * x) + 0.5


def _silu(x):
    h = 0.5 * x
    return h * jnp.tanh(h) + h


def _rmsnorm(x, g):
    ms = jnp.mean(x * x, axis=-1, keepdims=True)
    return x * lax.rsqrt(ms + EPS) * g


def _layernorm(x, g, b):
    mu = jnp.mean(x, axis=-1, keepdims=True)
    xc = x - mu
    var = jnp.mean(xc * xc, axis=-1, keepdims=True)
    return xc * lax.rsqrt(var + LN_EPS) * g + b


def _lanes(c):
    return slice(c * LANES, (c + 1) * LANES)


def _pipelined_step(load_xf, load_xb, load_pb, store_y, weights, vecs, bufs, convs):
    w_in, w_out_a, w_out_b, w_o, w_pe, w_pg = weights
    g_mix, b_conv_b, ln_g, ln_b, g_ple, g_final = vecs
    u_ref, ca_buf, cb_buf, sbz_ref, gaya_ref, sgb_ref, ybpre_ref, yapre_ref = bufs

    def dot(a, w):
        return jnp.dot(a, w, preferred_element_type=F32)

    def proj(j):
        col = (2 * len(PAIRS) + SINGLES.index(j)) * D
        return dot(u_ref[...], w_in[:, col:col + D])

    def proj_pair(j1, j2):
        col = PAIRS.index((j1, j2)) * 2 * D
        r = dot(u_ref[...], w_in[:, col:col + 2 * D])
        return [(r[:, 2 * c * LANES:(2 * c + 1) * LANES],
                 r[:, (2 * c + 1) * LANES:(2 * c + 2) * LANES]) for c in range(N_SLAB)]

    u_ref[...] = _rmsnorm(load_xf(), g_mix[...]).astype(BF16)

    cb = cb_buf[...] + b_conv_b[...]
    yb_pre = _silu(_layernorm(cb, ln_g[...], ln_b[...])) * sbz_ref[...]
    ybpre_ref[...] = yb_pre.astype(BF16)

    convs.put_v([bv * _sigmoid(bg) for bv, bg in proj_pair(P_BV, P_BG)])
    convs.put_s([ac * ah for ac, ah in proj_pair(P_AC, P_AH)])
    convs.conv_a()
    for c, (ab, az) in enumerate(proj_pair(P_AB, P_AZ)):
        yapre_ref[:, _lanes(c)] = (ab * ca_buf[:, _lanes(c)] * _silu(az)).astype(BF16)
    ya = dot(yapre_ref[...], w_out_a[...])

    yb = dot(ybpre_ref[...], w_out_b[...])
    m = gaya_ref[...] + sgb_ref[...] * yb
    h = load_xb() + dot(m.astype(BF16), w_o[...])
    pe = dot(load_pb().astype(BF16), w_pe[...])
    r = _rmsnorm(h, g_ple[...]).astype(BF16)
    pg = _sigmoid(dot(r, w_pg[...]))
    h = h + pg * pe
    store_y(_rmsnorm(h, g_final[...]))

    convs.conv_b(range(0, N_SLAB // 2))
    sbz_ref[...] = _silu(proj(P_BZ))
    gaya_ref[...] = _sigmoid(proj(P_GA)) * ya
    sgb_ref[...] = _sigmoid(proj(P_GB))
    convs.conv_b(range(N_SLAB // 2, N_SLAB))


def _w_in_copy(w_hbm, w_vmem, sems, pos):
    j, c = W_IN_ORDER[pos]
    return pltpu.make_async_copy(w_hbm.at[:, pl.ds(j * D + c * LANES, LANES)],
                                 w_vmem.at[:, pl.ds(pos * LANES, LANES)], sems.at[pos])


def _first_step_setup(zero_refs, w_hbm, w_vmem, sems):
    @pl.when(pl.program_id(0) == 0)
    def _():
        for pos in range(len(W_IN_ORDER)):
            _w_in_copy(w_hbm, w_vmem, sems, pos).start()
        for ref in zero_refs:
            ref[...] = jnp.zeros(ref.shape, ref.dtype)
        for pos in range(len(W_IN_ORDER)):
            _w_in_copy(w_hbm, w_vmem, sems, pos).wait()


def _common_scratch(tm):
    return [pltpu.VMEM((tm, D), BF16),
            pltpu.VMEM((tm, D), F32),
            pltpu.VMEM((tm, D), F32),
            pltpu.VMEM((tm, D), F32),
            pltpu.VMEM((tm, D), F32),
            pltpu.VMEM((tm, D), F32),
            pltpu.VMEM((tm, D), BF16),
            pltpu.VMEM((tm, D), BF16),
            pltpu.VMEM((D, N_PROJ * D), BF16),
            pltpu.SemaphoreType.DMA((len(W_IN_ORDER),))]


N_BUFS = 8
N_COMMON = 10
N_WEIGHTS = 6
N_VECS = 6


def _split_consts(refs):
    a, b, c = N_WEIGHTS, N_WEIGHTS + 2, N_WEIGHTS + 2 + N_VECS
    return refs[:a], refs[a:b], refs[b:c], refs[c:]


class _PromptConvs:
    def __init__(self, slab_a, slab_b, ca_buf, cb_buf, w_conv_a, w_conv_b, tm, t):
        self.slab_a, self.slab_b, self.ca_buf, self.cb_buf = slab_a, slab_b, ca_buf, cb_buf
        self.w_conv_a, self.w_conv_b, self.tm, self.t = w_conv_a, w_conv_b, tm, t

    def _put(self, slab, hist, pieces):
        prev = slab[:, self.tm:self.tm + hist, :]
        slab[:, 0:hist, :] = jnp.where(self.t == 0, 0.0, prev)
        for c, piece in enumerate(pieces):
            slab[c, hist:hist + self.tm, :] = piece

    def put_v(self, v):
        self._put(self.slab_b, HIST_B, v)

    def put_s(self, s):
        self._put(self.slab_a, HIST_A, s)

    def _conv(self, slab, w_ref, out_ref, hist, k_taps, slabs=range(N_SLAB)):
        first = hist - (k_taps - 1)
        for c in slabs:
            for r0 in range(0, self.tm, CONV_ROWS):
                acc = None
                for k in range(k_taps):
                    start = first + r0 + k
                    term = slab[c, start:start + CONV_ROWS, :] * w_ref[k:k + 1, _lanes(c)]
                    acc = term if acc is None else acc + term
                out_ref[r0:r0 + CONV_ROWS, _lanes(c)] = acc

    def conv_a(self):
        self._conv(self.slab_a, self.w_conv_a, self.ca_buf, HIST_A, K_A)

    def conv_b(self, slabs=range(N_SLAB)):
        self._conv(self.slab_b, self.w_conv_b, self.cb_buf, HIST_B, K_B, slabs)


def _prompt_kernel(*refs, tm, tiles_per_seq, n_tiles):
    n_in = 3
    xf_ref, xb_ref, pb_ref = refs[:n_in]
    weights, (w_conv_a, w_conv_b), vecs, rest = _split_consts(refs[n_in:])
    y_ref, tail_a_ref, tail_b_ref = rest[:3]
    bufs, w_in_vmem, w_in_sems = rest[3:3 + N_BUFS], rest[3 + N_BUFS], rest[4 + N_BUFS]
    slab_a, slab_b = rest[3 + N_COMMON:]
    ca_buf, cb_buf, sbz_ref, gaya_ref, sgb_ref = bufs[1:6]

    tile = jnp.minimum(pl.program_id(0), n_tiles - 1)
    t = tile % tiles_per_seq
    _first_step_setup((slab_a, slab_b, cb_buf, sbz_ref, gaya_ref, sgb_ref),
                      weights[0], w_in_vmem, w_in_sems)

    def store_y(val):
        y_ref[...] = val

    convs = _PromptConvs(slab_a, slab_b, ca_buf, cb_buf, w_conv_a, w_conv_b, tm, t)
    _pipelined_step(lambda: xf_ref[...], lambda: xb_ref[...], lambda: pb_ref[...], store_y,
                    (w_in_vmem,) + tuple(weights[1:]), vecs, bufs, convs)

    @pl.when(t == tiles_per_seq - 1)
    def _():
        for c in range(N_SLAB):
            tail_a_ref[:, _lanes(c)] = slab_a[c, tm:tm + HIST_A, :]
            tail_b_ref[:, _lanes(c)] = slab_b[c, tm:tm + HIST_B, :]


class _SampleConvs:
    def __init__(self, st_a_ref, st_b_ref, s_ref, v_ref, ca_buf, cb_buf, w_conv_a, w_conv_b,
                 ns, t_len):
        self.st_a_ref, self.st_b_ref, self.s_ref, self.v_ref = st_a_ref, st_b_ref, s_ref, v_ref
        self.ca_buf, self.cb_buf = ca_buf, cb_buf
        self.w_conv_a, self.w_conv_b, self.ns, self.t_len = w_conv_a, w_conv_b, ns, t_len

    def put_v(self, pieces):
        for c, piece in enumerate(pieces):
            self.v_ref[:, _lanes(c)] = piece

    def put_s(self, pieces):
        for c, piece in enumerate(pieces):
            self.s_ref[:, _lanes(c)] = piece

    def _conv(self, st_ref, new_ref, w_ref, out_ref, k_taps, slabs=range(N_SLAB)):
        ns, t_len, hist = self.ns, self.t_len, k_taps - 1
        for c in slabs:
            w = [w_ref[k:k + 1, _lanes(c)] for k in range(k_taps)]
            for n0 in range(0, ns, SUBLANES):
                acc = [None] * t_len
                for e in range(hist + t_len):
                    if e < hist:
                        x = st_ref[e, n0:n0 + SUBLANES, _lanes(c)]
                    else:
                        r = (e - hist) * ns + n0
                        x = new_ref[r:r + SUBLANES, _lanes(c)]
                    for t in range(t_len):
                        k = e - t
                        if 0 <= k < k_taps:
                            term = x * w[k]
                            acc[t] = term if acc[t] is None else acc[t] + term
                for t in range(t_len):
                    out_ref[t * ns + n0:t * ns + n0 + SUBLANES, _lanes(c)] = acc[t]

    def conv_a(self):
        self._conv(self.st_a_ref, self.s_ref, self.w_conv_a, self.ca_buf, K_A)

    def conv_b(self, slabs=range(N_SLAB)):
        self._conv(self.st_b_ref, self.v_ref, self.w_conv_b, self.cb_buf, K_B, slabs)


def _sample_kernel(*refs, ns, t_len):
    n_in = 5
    xf_ref, xb_ref, pb_ref, st_a_ref, st_b_ref = refs[:n_in]
    weights, (w_conv_a, w_conv_b), vecs, rest = _split_consts(refs[n_in:])
    y_ref, new_a_ref, new_b_ref = rest[:3]
    bufs, w_in_vmem, w_in_sems = rest[3:3 + N_BUFS], rest[3 + N_BUFS], rest[4 + N_BUFS]
    s_ref, v_ref = rest[3 + N_COMMON:]
    ca_buf, cb_buf, sbz_ref, gaya_ref, sgb_ref = bufs[1:6]
    rows = ns * t_len

    _first_step_setup((cb_buf, sbz_ref, gaya_ref, sgb_ref), weights[0], w_in_vmem, w_in_sems)

    def store_y(val):
        y_ref[...] = val.reshape(t_len, ns, D)

    convs = _SampleConvs(st_a_ref, st_b_ref, s_ref, v_ref, ca_buf, cb_buf, w_conv_a, w_conv_b,
                         ns, t_len)
    _pipelined_step(lambda: xf_ref[...].reshape(rows, D), lambda: xb_ref[...].reshape(rows, D),
                    lambda: pb_ref[...].reshape(rows, PLE), store_y,
                    (w_in_vmem,) + tuple(weights[1:]), vecs, bufs, convs)

    for e in range(K_A - 1):
        src = e + t_len - (K_A - 1)
        new_a_ref[e] = s_ref[src * ns:(src + 1) * ns, :]
    for e in range(K_B - 1):
        src = e + t_len
        if src < K_B - 1:
            new_b_ref[e] = st_b_ref[src]
        else:
            src -= K_B - 1
            new_b_ref[e] = v_ref[src * ns:(src + 1) * ns, :]


def _whole(memory_space=pltpu.VMEM):
    return pl.BlockSpec(memory_space=memory_space)


def kernel(x_prompt, x_sample, state_conv_a, state_conv_b, p_prompt, p_sample, g_mix, w_in, w_conv_a, w_out_a, w_conv_b, b_conv_b, ln_g, ln_b, w_out_b, w_o, w_pe, g_ple, w_pg, g_final):
    depth = w_in.shape[0]
    assert depth == 1, "single-layer trunk"
    n_p, seq, d = x_prompt.shape
    n_s, t_s, _ = x_sample.shape
    assert d == D and w_in.shape[2] == N_PROJ * D
    assert t_s >= K_A - 1 and t_s <= K_B - 1

    consts = (w_in[0].astype(BF16),
              w_out_a[0].astype(BF16), w_out_b[0].astype(BF16),
              w_o[0].astype(BF16), w_pe[0].astype(BF16), w_pg[0].astype(BF16),
              w_conv_a[0], w_conv_b[0],
              g_mix[0][None], b_conv_b[0][None], ln_g[0][None], ln_b[0][None],
              g_ple[0][None], g_final[None])
    const_specs = [_whole(pl.ANY)] + [_whole() for _ in consts[1:]]
    params = pltpu.CompilerParams(dimension_semantics=("arbitrary",),
                                  vmem_limit_bytes=VMEM_LIMIT_BYTES)

    tm = 256
    assert seq % tm == 0 and tm >= HIST_B and tm % CONV_ROWS == 0
    tiles_per_seq = seq // tm
    n_tiles = n_p * tiles_per_seq

    def front(i):
        return (jnp.minimum(i, n_tiles - 1), 0)

    def back(i):
        return (jnp.maximum(i - 1, 0), 0)

    def seq_of_front(i):
        return (jnp.minimum(i, n_tiles - 1) // tiles_per_seq, 0, 0)

    x_flat = x_prompt.reshape(n_p * seq, D)
    y_p, tail_a, tail_b = pl.pallas_call(
        functools.partial(_prompt_kernel, tm=tm, tiles_per_seq=tiles_per_seq, n_tiles=n_tiles),
        grid=(n_tiles + 1,),
        in_specs=[pl.BlockSpec((tm, D), front),
                  pl.BlockSpec((tm, D), back),
                  pl.BlockSpec((tm, PLE), back)] + const_specs,
        out_specs=[pl.BlockSpec((tm, D), back),
                   pl.BlockSpec((None, HIST_A, D), seq_of_front),
                   pl.BlockSpec((None, HIST_B, D), seq_of_front)],
        out_shape=[jax.ShapeDtypeStruct((n_p * seq, D), F32),
                   jax.ShapeDtypeStruct((n_p, HIST_A, D), F32),
                   jax.ShapeDtypeStruct((n_p, HIST_B, D), F32)],
        scratch_shapes=_common_scratch(tm) + [
            pltpu.VMEM((N_SLAB, HIST_A + tm, LANES), F32),
            pltpu.VMEM((N_SLAB, HIST_B + tm, LANES), F32)],
        compiler_params=params,
        name="prompt_layer",
    )(x_flat, x_flat, p_prompt[0].reshape(n_p * seq, PLE), *consts)
    y_p = y_p.reshape(n_p, seq, D)
    new_a_p = tail_a[:, HIST_A - (K_A - 1):, :][None]
    new_b_p = tail_b[:, HIST_B - (K_B - 1):, :][None]

    ns = 16
    assert n_s % ns == 0 and ns % SUBLANES == 0
    n_tiles_s = n_s // ns

    def front_s(i):
        return (0, jnp.minimum(i, n_tiles_s - 1), 0)

    def back_s(i):
        return (0, jnp.maximum(i - 1, 0), 0)

    def t_major(a):
        return jnp.transpose(a, (1, 0, 2))

    x_t = t_major(x_sample)
    y_t, new_a_t, new_b_t = pl.pallas_call(
        functools.partial(_sample_kernel, ns=ns, t_len=t_s),
        grid=(n_tiles_s + 1,),
        in_specs=[pl.BlockSpec((t_s, ns, D), front_s),
                  pl.BlockSpec((t_s, ns, D), back_s),
                  pl.BlockSpec((t_s, ns, PLE), back_s),
                  pl.BlockSpec((K_A - 1, ns, D), front_s),
                  pl.BlockSpec((K_B - 1, ns, D), front_s)] + const_specs,
        out_specs=[pl.BlockSpec((t_s, ns, D), back_s),
                   pl.BlockSpec((K_A - 1, ns, D), front_s),
                   pl.BlockSpec((K_B - 1, ns, D), front_s)],
        out_shape=[jax.ShapeDtypeStruct((t_s, n_s, D), F32),
                   jax.ShapeDtypeStruct((K_A - 1, n_s, D), F32),
                   jax.ShapeDtypeStruct((K_B - 1, n_s, D), F32)],
        scratch_shapes=_common_scratch(ns * t_s) + [
            pltpu.VMEM((ns * t_s, D), F32),
            pltpu.VMEM((ns * t_s, D), F32)],
        compiler_params=params,
        name="sample_layer",
    )(x_t, x_t, t_major(p_sample[0]), t_major(state_conv_a[0]), t_major(state_conv_b[0]),
      *consts)
    return (y_p, t_major(y_t), new_a_p, new_b_p, t_major(new_a_t)[None], t_major(new_b_t)[None])
```

```python
import functools

import jax
import jax.numpy as jnp
from jax import lax
from jax.experimental import pallas as pl
from jax.experimental.pallas import tpu as pltpu

D = 1024
N_PROJ = 9
PLE = 256
K_A = 3
K_B = 31
EPS = 1e-6
LN_EPS = 1e-5
LANES = 128
SUBLANES = 8
N_SLAB = D // LANES
HIST_A = SUBLANES
HIST_B = -(-(K_B - 1) // SUBLANES) * SUBLANES
CONV_ROWS = 64
VMEM_LIMIT_BYTES = 60 * 1024 * 1024

P_AB, P_AC, P_AH, P_AZ, P_BV, P_BG, P_BZ, P_GA, P_GB = range(N_PROJ)
PAIRS = ((P_BV, P_BG), (P_AC, P_AH), (P_AB, P_AZ))
SINGLES = (P_BZ, P_GA, P_GB)
W_IN_ORDER = (tuple((j, c) for pair in PAIRS for c in range(D // 128) for j in pair)
              + tuple((j, c) for j in SINGLES for c in range(D // 128)))

F32 = jnp.float32
BF16 = jnp.bfloat16


def _sigmoid(x):
    return 0.5 * jnp.tanh(0.5 * x) + 0.5


def _silu(x):
    h = 0.5 * x
    return h * jnp.tanh(h) + h


def _rmsnorm(x, g):
    ms = jnp.mean(x * x, axis=-1, keepdims=True)
    return x * lax.rsqrt(ms + EPS) * g


def _layernorm(x, g, b):
    mu = jnp.mean(x, axis=-1, keepdims=True)
    xc = x - mu
    var = jnp.mean(xc * xc, axis=-1, keepdims=True)
    return xc * lax.rsqrt(var + LN_EPS) * g + b


def _lanes(c):
    return slice(c * LANES, (c + 1) * LANES)


def _pipelined_step(load_xn, load_xb, load_pb, store_y, weights, vecs, bufs, convs):
    w_in, w_out_a, w_out_b, w_o, w_pe, w_pg = weights
    g_mix, b_conv_b, ln_g, ln_b, g_ple, g_final = vecs
    u_ref, ca_buf, cb_buf, sbz_ref, gaya_ref, sgb_ref, ybpre_ref, yapre_ref, unext_ref = bufs

    def dot(a, w):
        return jnp.dot(a, w, preferred_element_type=F32)

    def proj(j):
        col = (2 * len(PAIRS) + SINGLES.index(j)) * D
        return dot(u_ref[...], w_in[:, col:col + D])

    def proj_pair(j1, j2):
        col = PAIRS.index((j1, j2)) * 2 * D
        r = dot(u_ref[...], w_in[:, col:col + 2 * D])
        return [(r[:, 2 * c * LANES:(2 * c + 1) * LANES],
                 r[:, (2 * c + 1) * LANES:(2 * c + 2) * LANES]) for c in range(N_SLAB)]

    cb = cb_buf[...] + b_conv_b[...]
    yb_pre = _silu(_layernorm(cb, ln_g[...], ln_b[...])) * sbz_ref[...]
    ybpre_ref[...] = yb_pre.astype(BF16)

    convs.put_v([bv * _sigmoid(bg) for bv, bg in proj_pair(P_BV, P_BG)])
    convs.put_s([ac * ah for ac, ah in proj_pair(P_AC, P_AH)])
    convs.conv_a()
    for c, (ab, az) in enumerate(proj_pair(P_AB, P_AZ)):
        yapre_ref[:, _lanes(c)] = (ab * ca_buf[:, _lanes(c)] * _silu(az)).astype(BF16)
    ya = dot(yapre_ref[...], w_out_a[...])

    yb = dot(ybpre_ref[...], w_out_b[...])
    m = gaya_ref[...] + sgb_ref[...] * yb
    h = load_xb() + dot(m.astype(BF16), w_o[...])
    pe = dot(load_pb().astype(BF16), w_pe[...])
    r = _rmsnorm(h, g_ple[...]).astype(BF16)
    pg = _sigmoid(dot(r, w_pg[...]))
    h = h + pg * pe
    store_y(_rmsnorm(h, g_final[...]))

    unext_ref[...] = _rmsnorm(load_xn(), g_mix[...]).astype(BF16)

    convs.conv_b(range(0, N_SLAB // 2))
    sbz_ref[...] = _silu(proj(P_BZ))
    gaya_ref[...] = _sigmoid(proj(P_GA)) * ya
    sgb_ref[...] = _sigmoid(proj(P_GB))
    convs.conv_b(range(N_SLAB // 2, N_SLAB))
    u_ref[...] = unext_ref[...]


def _w_in_copy(w_hbm, w_vmem, sems, pos):
    j, c = W_IN_ORDER[pos]
    return pltpu.make_async_copy(w_hbm.at[:, pl.ds(j * D + c * LANES, LANES)],
                                 w_vmem.at[:, pl.ds(pos * LANES, LANES)], sems.at[pos])


def _first_step_setup(zero_refs, w_hbm, w_vmem, sems, u_ref, load_x0, g_mix):
    @pl.when(pl.program_id(0) == 0)
    def _():
        u_ref[...] = _rmsnorm(load_x0(), g_mix[...]).astype(BF16)
        for pos in range(len(W_IN_ORDER)):
            _w_in_copy(w_hbm, w_vmem, sems, pos).start()
        for ref in zero_refs:
            ref[...] = jnp.zeros(ref.shape, ref.dtype)
        for pos in range(len(W_IN_ORDER)):
            _w_in_copy(w_hbm, w_vmem, sems, pos).wait()


def _common_scratch(tm):
    return [pltpu.VMEM((tm, D), BF16),
            pltpu.VMEM((tm, D), F32),
            pltpu.VMEM((tm, D), F32),
            pltpu.VMEM((tm, D), F32),
            pltpu.VMEM((tm, D), F32),
            pltpu.VMEM((tm, D), F32),
            pltpu.VMEM((tm, D), BF16),
            pltpu.VMEM((tm, D), BF16),
            pltpu.VMEM((tm, D), BF16),
            pltpu.VMEM((D, N_PROJ * D), BF16),
            pltpu.SemaphoreType.DMA((len(W_IN_ORDER),))]


N_BUFS = 9
N_COMMON = 11
N_WEIGHTS = 6
N_VECS = 6


def _split_consts(refs):
    a, b, c = N_WEIGHTS, N_WEIGHTS + 2, N_WEIGHTS + 2 + N_VECS
    return refs[:a], refs[a:b], refs[b:c], refs[c:]


class _PromptConvs:
    def __init__(self, slab_a, slab_b, ca_buf, cb_buf, w_conv_a, w_conv_b, tm, t):
        self.slab_a, self.slab_b, self.ca_buf, self.cb_buf = slab_a, slab_b, ca_buf, cb_buf
        self.w_conv_a, self.w_conv_b, self.tm, self.t = w_conv_a, w_conv_b, tm, t

    def _put(self, slab, hist, pieces):
        prev = slab[:, self.tm:self.tm + hist, :]
        slab[:, 0:hist, :] = jnp.where(self.t == 0, 0.0, prev)
        for c, piece in enumerate(pieces):
            slab[c, hist:hist + self.tm, :] = piece

    def put_v(self, v):
        self._put(self.slab_b, HIST_B, v)

    def put_s(self, s):
        self._put(self.slab_a, HIST_A, s)

    def _conv(self, slab, w_ref, out_ref, hist, k_taps, slabs=range(N_SLAB)):
        first = hist - (k_taps - 1)
        for c in slabs:
            for r0 in range(0, self.tm, CONV_ROWS):
                acc = None
                for k in range(k_taps):
                    start = first + r0 + k
                    term = slab[c, start:start + CONV_ROWS, :] * w_ref[k:k + 1, _lanes(c)]
                    acc = term if acc is None else acc + term
                out_ref[r0:r0 + CONV_ROWS, _lanes(c)] = acc

    def conv_a(self):
        self._conv(self.slab_a, self.w_conv_a, self.ca_buf, HIST_A, K_A)

    def conv_b(self, slabs=range(N_SLAB)):
        self._conv(self.slab_b, self.w_conv_b, self.cb_buf, HIST_B, K_B, slabs)


def _prompt_kernel(*refs, tm, tiles_per_seq, n_tiles):
    n_in = 3
    xn_ref, xb_ref, pb_ref = refs[:n_in]
    weights, (w_conv_a, w_conv_b), vecs, rest = _split_consts(refs[n_in:])
    y_ref, tail_a_ref, tail_b_ref = rest[:3]
    bufs, w_in_vmem, w_in_sems = rest[3:3 + N_BUFS], rest[3 + N_BUFS], rest[4 + N_BUFS]
    slab_a, slab_b = rest[3 + N_COMMON:]
    ca_buf, cb_buf, sbz_ref, gaya_ref, sgb_ref = bufs[1:6]

    tile = jnp.minimum(pl.program_id(0), n_tiles - 1)
    t = tile % tiles_per_seq
    _first_step_setup((slab_a, slab_b, cb_buf, sbz_ref, gaya_ref, sgb_ref),
                      weights[0], w_in_vmem, w_in_sems, bufs[0], lambda: xb_ref[...], vecs[0])

    def store_y(val):
        y_ref[...] = val

    convs = _PromptConvs(slab_a, slab_b, ca_buf, cb_buf, w_conv_a, w_conv_b, tm, t)
    _pipelined_step(lambda: xn_ref[...], lambda: xb_ref[...], lambda: pb_ref[...], store_y,
                    (w_in_vmem,) + tuple(weights[1:]), vecs, bufs, convs)

    @pl.when(t == tiles_per_seq - 1)
    def _():
        for c in range(N_SLAB):
            tail_a_ref[:, _lanes(c)] = slab_a[c, tm:tm + HIST_A, :]
            tail_b_ref[:, _lanes(c)] = slab_b[c, tm:tm + HIST_B, :]


class _SampleConvs:
    def __init__(self, st_a_ref, st_b_ref, s_ref, v_ref, ca_buf, cb_buf, w_conv_a, w_conv_b,
                 ns, t_len):
        self.st_a_ref, self.st_b_ref, self.s_ref, self.v_ref = st_a_ref, st_b_ref, s_ref, v_ref
        self.ca_buf, self.cb_buf = ca_buf, cb_buf
        self.w_conv_a, self.w_conv_b, self.ns, self.t_len = w_conv_a, w_conv_b, ns, t_len

    def put_v(self, pieces):
        for c, piece in enumerate(pieces):
            self.v_ref[:, _lanes(c)] = piece

    def put_s(self, pieces):
        for c, piece in enumerate(pieces):
            self.s_ref[:, _lanes(c)] = piece

    def _conv(self, st_ref, new_ref, w_ref, out_ref, k_taps, slabs=range(N_SLAB)):
        ns, t_len, hist = self.ns, self.t_len, k_taps - 1
        for c in slabs:
            w = [w_ref[k:k + 1, _lanes(c)] for k in range(k_taps)]
            for n0 in range(0, ns, SUBLANES):
                acc = [None] * t_len
                for e in range(hist + t_len):
                    if e < hist:
                        x = st_ref[e, n0:n0 + SUBLANES, _lanes(c)]
                    else:
                        r = (e - hist) * ns + n0
                        x = new_ref[r:r + SUBLANES, _lanes(c)]
                    for t in range(t_len):
                        k = e - t
                        if 0 <= k < k_taps:
                            term = x * w[k]
                            acc[t] = term if acc[t] is None else acc[t] + term
                for t in range(t_len):
                    out_ref[t * ns + n0:t * ns + n0 + SUBLANES, _lanes(c)] = acc[t]

    def conv_a(self):
        self._conv(self.st_a_ref, self.s_ref, self.w_conv_a, self.ca_buf, K_A)

    def conv_b(self, slabs=range(N_SLAB)):
        self._conv(self.st_b_ref, self.v_ref, self.w_conv_b, self.cb_buf, K_B, slabs)


def _sample_kernel(*refs, ns, t_len):
    n_in = 5
    xn_ref, xb_ref, pb_ref, st_a_ref, st_b_ref = refs[:n_in]
    weights, (w_conv_a, w_conv_b), vecs, rest = _split_consts(refs[n_in:])
    y_ref, new_a_ref, new_b_ref = rest[:3]
    bufs, w_in_vmem, w_in_sems = rest[3:3 + N_BUFS], rest[3 + N_BUFS], rest[4 + N_BUFS]
    s_ref, v_ref = rest[3 + N_COMMON:]
    ca_buf, cb_buf, sbz_ref, gaya_ref, sgb_ref = bufs[1:6]
    rows = ns * t_len

    _first_step_setup((cb_buf, sbz_ref, gaya_ref, sgb_ref), weights[0], w_in_vmem, w_in_sems,
                      bufs[0], lambda: xb_ref[...].reshape(rows, D), vecs[0])

    def store_y(val):
        y_ref[...] = val.reshape(t_len, ns, D)

    convs = _SampleConvs(st_a_ref, st_b_ref, s_ref, v_ref, ca_buf, cb_buf, w_conv_a, w_conv_b,
                         ns, t_len)
    _pipelined_step(lambda: xn_ref[...].reshape(rows, D), lambda: xb_ref[...].reshape(rows, D),
                    lambda: pb_ref[...].reshape(rows, PLE), store_y,
                    (w_in_vmem,) + tuple(weights[1:]), vecs, bufs, convs)

    for e in range(K_A - 1):
        src = e + t_len - (K_A - 1)
        new_a_ref[e] = s_ref[src * ns:(src + 1) * ns, :]
    for e in range(K_B - 1):
        src = e + t_len
        if src < K_B - 1:
            new_b_ref[e] = st_b_ref[src]
        else:
            src -= K_B - 1
            new_b_ref[e] = v_ref[src * ns:(src + 1) * ns, :]


def _whole(memory_space=pltpu.VMEM):
    return pl.BlockSpec(memory_space=memory_space)


def kernel(x_prompt, x_sample, state_conv_a, state_conv_b, p_prompt, p_sample, g_mix, w_in, w_conv_a, w_out_a, w_conv_b, b_conv_b, ln_g, ln_b, w_out_b, w_o, w_pe, g_ple, w_pg, g_final):
    depth = w_in.shape[0]
    assert depth == 1, "single-layer trunk"
    n_p, seq, d = x_prompt.shape
    n_s, t_s, _ = x_sample.shape
    assert d == D and w_in.shape[2] == N_PROJ * D
    assert t_s >= K_A - 1 and t_s <= K_B - 1

    consts = (w_in[0].astype(BF16),
              w_out_a[0].astype(BF16), w_out_b[0].astype(BF16),
              w_o[0].astype(BF16), w_pe[0].astype(BF16), w_pg[0].astype(BF16),
              w_conv_a[0], w_conv_b[0],
              g_mix[0][None], b_conv_b[0][None], ln_g[0][None], ln_b[0][None],
              g_ple[0][None], g_final[None])
    const_specs = [_whole(pl.ANY)] + [_whole() for _ in consts[1:]]
    params = pltpu.CompilerParams(dimension_semantics=("arbitrary",),
                                  vmem_limit_bytes=VMEM_LIMIT_BYTES)

    tm = 256
    assert seq % tm == 0 and tm >= HIST_B and tm % CONV_ROWS == 0
    tiles_per_seq = seq // tm
    n_tiles = n_p * tiles_per_seq

    def nxt(i):
        return (jnp.minimum(i + 1, n_tiles - 1), 0)

    def back(i):
        return (jnp.maximum(i - 1, 0), 0)

    def seq_of_front(i):
        return (jnp.minimum(i, n_tiles - 1) // tiles_per_seq, 0, 0)

    x_flat = x_prompt.reshape(n_p * seq, D)
    y_p, tail_a, tail_b = pl.pallas_call(
        functools.partial(_prompt_kernel, tm=tm, tiles_per_seq=tiles_per_seq, n_tiles=n_tiles),
        grid=(n_tiles + 1,),
        in_specs=[pl.BlockSpec((tm, D), nxt),
                  pl.BlockSpec((tm, D), back),
                  pl.BlockSpec((tm, PLE), back)] + const_specs,
        out_specs=[pl.BlockSpec((tm, D), back),
                   pl.BlockSpec((None, HIST_A, D), seq_of_front),
                   pl.BlockSpec((None, HIST_B, D), seq_of_front)],
        out_shape=[jax.ShapeDtypeStruct((n_p * seq, D), F32),
                   jax.ShapeDtypeStruct((n_p, HIST_A, D), F32),
                   jax.ShapeDtypeStruct((n_p, HIST_B, D), F32)],
        scratch_shapes=_common_scratch(tm) + [
            pltpu.VMEM((N_SLAB, HIST_A + tm, LANES), F32),
            pltpu.VMEM((N_SLAB, HIST_B + tm, LANES), F32)],
        compiler_params=params,
        name="prompt_layer",
    )(x_flat, x_flat, p_prompt[0].reshape(n_p * seq, PLE), *consts)
    y_p = y_p.reshape(n_p, seq, D)
    new_a_p = tail_a[:, HIST_A - (K_A - 1):, :][None]
    new_b_p = tail_b[:, HIST_B - (K_B - 1):, :][None]

    ns = 16
    assert n_s % ns == 0 and ns % SUBLANES == 0
    n_tiles_s = n_s // ns

    def front_s(i):
        return (0, jnp.minimum(i, n_tiles_s - 1), 0)

    def next_s(i):
        return (0, jnp.minimum(i + 1, n_tiles_s - 1), 0)

    def back_s(i):
        return (0, jnp.maximum(i - 1, 0), 0)

    def t_major(a):
        return jnp.transpose(a, (1, 0, 2))

    x_t = t_major(x_sample)
    y_t, new_a_t, new_b_t = pl.pallas_call(
        functools.partial(_sample_kernel, ns=ns, t_len=t_s),
        grid=(n_tiles_s + 1,),
        in_specs=[pl.BlockSpec((t_s, ns, D), next_s),
                  pl.BlockSpec((t_s, ns, D), back_s),
                  pl.BlockSpec((t_s, ns, PLE), back_s),
                  pl.BlockSpec((K_A - 1, ns, D), front_s),
                  pl.BlockSpec((K_B - 1, ns, D), front_s)] + const_specs,
        out_specs=[pl.BlockSpec((t_s, ns, D), back_s),
                   pl.BlockSpec((K_A - 1, ns, D), front_s),
                   pl.BlockSpec((K_B - 1, ns, D), front_s)],
        out_shape=[jax.ShapeDtypeStruct((t_s, n_s, D), F32),
                   jax.ShapeDtypeStruct((K_A - 1, n_s, D), F32),
                   jax.ShapeDtypeStruct((K_B - 1, n_s, D), F32)],
        scratch_shapes=_common_scratch(ns * t_s) + [
            pltpu.VMEM((ns * t_s, D), F32),
            pltpu.VMEM((ns * t_s, D), F32)],
        compiler_params=params,
        name="sample_layer",
    )(x_t, x_t, t_major(p_sample[0]), t_major(state_conv_a[0]), t_major(state_conv_b[0]),
      *consts)
    return (y_p, t_major(y_t), new_a_p, new_b_p, t_major(new_a_t)[None], t_major(new_b_t)[None])
```

```python
import functools

import jax
import jax.numpy as jnp
from jax import lax
from jax.experimental import pallas as pl
from jax.experimental.pallas import tpu as pltpu

D = 1024
N_PROJ = 9
PLE = 256
K_A = 3
K_B = 31
EPS = 1e-6
LN_EPS = 1e-5
LANES = 128
SUBLANES = 8
N_SLAB = D // LANES
HIST_A = SUBLANES
HIST_B = -(-(K_B - 1) // SUBLANES) * SUBLANES
CONV_ROWS = 64
VMEM_LIMIT_BYTES = 62 * 1024 * 1024

P_AB, P_AC, P_AH, P_AZ, P_BV, P_BG, P_BZ, P_GA, P_GB = range(N_PROJ)
PAIRS = ((P_BV, P_BG), (P_AC, P_AH), (P_AB, P_AZ))
SINGLES = (P_BZ, P_GA, P_GB)
W_IN_ORDER = (tuple((j, c) for pair in PAIRS for c in range(D // 128) for j in pair)
              + tuple((j, c) for j in SINGLES for c in range(D // 128)))

F32 = jnp.float32
BF16 = jnp.bfloat16


def _sigmoid(x):
    return 0.5 * jnp.tanh(0.5 * x) + 0.5


def _silu(x):
    h = 0.5 * x
    return h * jnp.tanh(h) + h


def _rmsnorm(x, g):
    ms = jnp.mean(x * x, axis=-1, keepdims=True)
    return x * lax.rsqrt(ms + EPS) * g


def _layernorm(x, g, b):
    mu = jnp.mean(x, axis=-1, keepdims=True)
    xc = x - mu
    var = jnp.mean(xc * xc, axis=-1, keepdims=True)
    return xc * lax.rsqrt(var + LN_EPS) * g + b


def _lanes(c):
    return slice(c * LANES, (c + 1) * LANES)


def _pipelined_step(load_xf, load_xb, load_pb, store_y, weights, vecs, bufs, convs):
    w_in, w_out_a, w_out_b, w_o, w_pe, w_pg = weights
    g_mix, b_conv_b, ln_g, ln_b, g_ple, g_final = vecs
    u_ref, cb_buf, sbz_ref, gaya_ref, sgb_ref, ybpre_ref, yapre_ref = bufs

    def dot(a, w):
        return jnp.dot(a, w, preferred_element_type=F32)

    def proj(j):
        col = (2 * len(PAIRS) + SINGLES.index(j)) * D
        return dot(u_ref[...], w_in[:, col:col + D])

    def proj_pair(j1, j2):
        col = PAIRS.index((j1, j2)) * 2 * D
        r = dot(u_ref[...], w_in[:, col:col + 2 * D])
        return [(r[:, 2 * c * LANES:(2 * c + 1) * LANES],
                 r[:, (2 * c + 1) * LANES:(2 * c + 2) * LANES]) for c in range(N_SLAB)]

    u_ref[...] = _rmsnorm(load_xf(), g_mix[...]).astype(BF16)

    cb = cb_buf[...] + b_conv_b[...]
    yb_pre = _silu(_layernorm(cb, ln_g[...], ln_b[...])) * sbz_ref[...]
    ybpre_ref[...] = yb_pre.astype(BF16)

    convs.put_v([bv * _sigmoid(bg) for bv, bg in proj_pair(P_BV, P_BG)])
    convs.put_s([ac * ah for ac, ah in proj_pair(P_AC, P_AH)])
    convs.conv_a()
    for c, (ab, az) in enumerate(proj_pair(P_AB, P_AZ)):
        yapre_ref[:, _lanes(c)] = (ab * cb_buf[:, _lanes(c)] * _silu(az)).astype(BF16)
    ya = dot(yapre_ref[...], w_out_a[...])

    yb = dot(ybpre_ref[...], w_out_b[...])
    m = gaya_ref[...] + sgb_ref[...] * yb
    h = load_xb() + dot(m.astype(BF16), w_o[...])
    pe = dot(load_pb().astype(BF16), w_pe[...])
    r = _rmsnorm(h, g_ple[...]).astype(BF16)
    pg = _sigmoid(dot(r, w_pg[...]))
    h = h + pg * pe
    store_y(_rmsnorm(h, g_final[...]))

    convs.conv_b(range(0, N_SLAB // 2))
    sbz_ref[...] = _silu(proj(P_BZ))
    gaya_ref[...] = _sigmoid(proj(P_GA)) * ya
    sgb_ref[...] = _sigmoid(proj(P_GB))
    convs.conv_b(range(N_SLAB // 2, N_SLAB))


def _w_in_copy(w_hbm, w_vmem, sems, pos):
    j, c = W_IN_ORDER[pos]
    return pltpu.make_async_copy(w_hbm.at[:, pl.ds(j * D + c * LANES, LANES)],
                                 w_vmem.at[:, pl.ds(pos * LANES, LANES)], sems.at[pos])


def _first_step_setup(zero_refs, w_hbm, w_vmem, sems):
    @pl.when(pl.program_id(0) == 0)
    def _():
        for pos in range(len(W_IN_ORDER)):
            _w_in_copy(w_hbm, w_vmem, sems, pos).start()
        for ref in zero_refs:
            ref[...] = jnp.zeros(ref.shape, ref.dtype)
        for pos in range(len(W_IN_ORDER)):
            _w_in_copy(w_hbm, w_vmem, sems, pos).wait()


def _common_scratch(tm):
    return [pltpu.VMEM((tm, D), BF16),
            pltpu.VMEM((tm, D), F32),
            pltpu.VMEM((tm, D), F32),
            pltpu.VMEM((tm, D), F32),
            pltpu.VMEM((tm, D), F32),
            pltpu.VMEM((tm, D), BF16),
            pltpu.VMEM((tm, D), BF16),
            pltpu.VMEM((D, N_PROJ * D), BF16),
            pltpu.SemaphoreType.DMA((len(W_IN_ORDER),))]


N_BUFS = 7
N_COMMON = 9
N_WEIGHTS = 6
N_VECS = 6


def _split_consts(refs):
    a, b, c = N_WEIGHTS, N_WEIGHTS + 2, N_WEIGHTS + 2 + N_VECS
    return refs[:a], refs[a:b], refs[b:c], refs[c:]


class _PromptConvs:
    def __init__(self, slab_a, slab_b, cb_buf, w_conv_a, w_conv_b, tm, t):
        self.slab_a, self.slab_b, self.cb_buf = slab_a, slab_b, cb_buf
        self.w_conv_a, self.w_conv_b, self.tm, self.t = w_conv_a, w_conv_b, tm, t

    def _put(self, slab, hist, pieces):
        prev = slab[:, self.tm:self.tm + hist, :]
        slab[:, 0:hist, :] = jnp.where(self.t == 0, 0.0, prev)
        for c, piece in enumerate(pieces):
            slab[c, hist:hist + self.tm, :] = piece

    def put_v(self, v):
        self._put(self.slab_b, HIST_B, v)

    def put_s(self, s):
        self._put(self.slab_a, HIST_A, s)

    def _conv(self, slab, w_ref, hist, k_taps, slabs=range(N_SLAB)):
        first = hist - (k_taps - 1)
        for c in slabs:
            for r0 in range(0, self.tm, CONV_ROWS):
                acc = None
                for k in range(k_taps):
                    start = first + r0 + k
                    term = slab[c, start:start + CONV_ROWS, :] * w_ref[k:k + 1, _lanes(c)]
                    acc = term if acc is None else acc + term
                self.cb_buf[r0:r0 + CONV_ROWS, _lanes(c)] = acc

    def conv_a(self):
        self._conv(self.slab_a, self.w_conv_a, HIST_A, K_A)

    def conv_b(self, slabs=range(N_SLAB)):
        self._conv(self.slab_b, self.w_conv_b, HIST_B, K_B, slabs)


def _prompt_kernel(*refs, tm, tiles_per_seq, n_tiles):
    n_in = 3
    xf_ref, xb_ref, pb_ref = refs[:n_in]
    weights, (w_conv_a, w_conv_b), vecs, rest = _split_consts(refs[n_in:])
    y_ref, tail_a_ref, tail_b_ref = rest[:3]
    bufs, w_in_vmem, w_in_sems = rest[3:3 + N_BUFS], rest[3 + N_BUFS], rest[4 + N_BUFS]
    slab_a, slab_b = rest[3 + N_COMMON:]
    cb_buf, sbz_ref, gaya_ref, sgb_ref = bufs[1:5]

    tile = jnp.minimum(pl.program_id(0), n_tiles - 1)
    t = tile % tiles_per_seq
    _first_step_setup((slab_a, slab_b, cb_buf, sbz_ref, gaya_ref, sgb_ref),
                      weights[0], w_in_vmem, w_in_sems)

    def store_y(val):
        y_ref[...] = val

    convs = _PromptConvs(slab_a, slab_b, cb_buf, w_conv_a, w_conv_b, tm, t)
    _pipelined_step(lambda: xf_ref[...], lambda: xb_ref[...], lambda: pb_ref[...], store_y,
                    (w_in_vmem,) + tuple(weights[1:]), vecs, bufs, convs)

    @pl.when(t == tiles_per_seq - 1)
    def _():
        for c in range(N_SLAB):
            tail_a_ref[:, _lanes(c)] = slab_a[c, tm:tm + HIST_A, :]
            tail_b_ref[:, _lanes(c)] = slab_b[c, tm:tm + HIST_B, :]


class _SampleConvs:
    def __init__(self, st_a_ref, st_b_ref, s_ref, v_ref, cb_buf, w_conv_a, w_conv_b, ns, t_len):
        self.st_a_ref, self.st_b_ref, self.s_ref, self.v_ref = st_a_ref, st_b_ref, s_ref, v_ref
        self.cb_buf = cb_buf
        self.w_conv_a, self.w_conv_b, self.ns, self.t_len = w_conv_a, w_conv_b, ns, t_len

    def put_v(self, pieces):
        for c, piece in enumerate(pieces):
            self.v_ref[:, _lanes(c)] = piece

    def put_s(self, pieces):
        for c, piece in enumerate(pieces):
            self.s_ref[:, _lanes(c)] = piece

    def _conv(self, st_ref, new_ref, w_ref, k_taps, slabs=range(N_SLAB)):
        ns, t_len, hist = self.ns, self.t_len, k_taps - 1
        for c in slabs:
            w = [w_ref[k:k + 1, _lanes(c)] for k in range(k_taps)]
            for n0 in range(0, ns, SUBLANES):
                acc = [None] * t_len
                for e in range(hist + t_len):
                    if e < hist:
                        x = st_ref[e, n0:n0 + SUBLANES, _lanes(c)]
                    else:
                        r = (e - hist) * ns + n0
                        x = new_ref[r:r + SUBLANES, _lanes(c)]
                    for t in range(t_len):
                        k = e - t
                        if 0 <= k < k_taps:
                            term = x * w[k]
                            acc[t] = term if acc[t] is None else acc[t] + term
                for t in range(t_len):
                    self.cb_buf[t * ns + n0:t * ns + n0 + SUBLANES, _lanes(c)] = acc[t]

    def conv_a(self):
        self._conv(self.st_a_ref, self.s_ref, self.w_conv_a, K_A)

    def conv_b(self, slabs=range(N_SLAB)):
        self._conv(self.st_b_ref, self.v_ref, self.w_conv_b, K_B, slabs)


def _sample_kernel(*refs, ns, t_len):
    n_in = 5
    xf_ref, xb_ref, pb_ref, st_a_ref, st_b_ref = refs[:n_in]
    weights, (w_conv_a, w_conv_b), vecs, rest = _split_consts(refs[n_in:])
    y_ref, new_a_ref, new_b_ref = rest[:3]
    bufs, w_in_vmem, w_in_sems = rest[3:3 + N_BUFS], rest[3 + N_BUFS], rest[4 + N_BUFS]
    s_ref, v_ref = rest[3 + N_COMMON:]
    cb_buf, sbz_ref, gaya_ref, sgb_ref = bufs[1:5]
    rows = ns * t_len

    _first_step_setup((cb_buf, sbz_ref, gaya_ref, sgb_ref), weights[0], w_in_vmem, w_in_sems)

    def store_y(val):
        y_ref[...] = val.reshape(t_len, ns, D)

    convs = _SampleConvs(st_a_ref, st_b_ref, s_ref, v_ref, cb_buf, w_conv_a, w_conv_b, ns, t_len)
    _pipelined_step(lambda: xf_ref[...].reshape(rows, D), lambda: xb_ref[...].reshape(rows, D),
                    lambda: pb_ref[...].reshape(rows, PLE), store_y,
                    (w_in_vmem,) + tuple(weights[1:]), vecs, bufs, convs)

    for e in range(K_A - 1):
        src = e + t_len - (K_A - 1)
        new_a_ref[e] = s_ref[src * ns:(src + 1) * ns, :]
    for e in range(K_B - 1):
        src = e + t_len
        if src < K_B - 1:
            new_b_ref[e] = st_b_ref[src]
        else:
            src -= K_B - 1
            new_b_ref[e] = v_ref[src * ns:(src + 1) * ns, :]


def _whole(memory_space=pltpu.VMEM):
    return pl.BlockSpec(memory_space=memory_space)


def kernel(x_prompt, x_sample, state_conv_a, state_conv_b, p_prompt, p_sample, g_mix, w_in, w_conv_a, w_out_a, w_conv_b, b_conv_b, ln_g, ln_b, w_out_b, w_o, w_pe, g_ple, w_pg, g_final):
    depth = w_in.shape[0]
    assert depth == 1, "single-layer trunk"
    n_p, seq, d = x_prompt.shape
    n_s, t_s, _ = x_sample.shape
    assert d == D and w_in.shape[2] == N_PROJ * D
    assert t_s >= K_A - 1 and t_s <= K_B - 1

    consts = (w_in[0].astype(BF16),
              w_out_a[0].astype(BF16), w_out_b[0].astype(BF16),
              w_o[0].astype(BF16), w_pe[0].astype(BF16), w_pg[0].astype(BF16),
              w_conv_a[0], w_conv_b[0],
              g_mix[0][None], b_conv_b[0][None], ln_g[0][None], ln_b[0][None],
              g_ple[0][None], g_final[None])
    const_specs = [_whole(pl.ANY)] + [_whole() for _ in consts[1:]]
    params = pltpu.CompilerParams(dimension_semantics=("arbitrary",),
                                  vmem_limit_bytes=VMEM_LIMIT_BYTES)

    tm = 512
    assert seq % tm == 0 and tm >= HIST_B and tm % CONV_ROWS == 0
    tiles_per_seq = seq // tm
    n_tiles = n_p * tiles_per_seq

    def front(i):
        return (jnp.minimum(i, n_tiles - 1), 0)

    def back(i):
        return (jnp.maximum(i - 1, 0), 0)

    def seq_of_front(i):
        return (jnp.minimum(i, n_tiles - 1) // tiles_per_seq, 0, 0)

    x_flat = x_prompt.reshape(n_p * seq, D)
    y_p, tail_a, tail_b = pl.pallas_call(
        functools.partial(_prompt_kernel, tm=tm, tiles_per_seq=tiles_per_seq, n_tiles=n_tiles),
        grid=(n_tiles + 1,),
        in_specs=[pl.BlockSpec((tm, D), front),
                  pl.BlockSpec((tm, D), back),
                  pl.BlockSpec((tm, PLE), back)] + const_specs,
        out_specs=[pl.BlockSpec((tm, D), back),
                   pl.BlockSpec((None, HIST_A, D), seq_of_front),
                   pl.BlockSpec((None, HIST_B, D), seq_of_front)],
        out_shape=[jax.ShapeDtypeStruct((n_p * seq, D), F32),
                   jax.ShapeDtypeStruct((n_p, HIST_A, D), F32),
                   jax.ShapeDtypeStruct((n_p, HIST_B, D), F32)],
        scratch_shapes=_common_scratch(tm) + [
            pltpu.VMEM((N_SLAB, HIST_A + tm, LANES), F32),
            pltpu.VMEM((N_SLAB, HIST_B + tm, LANES), F32)],
        compiler_params=params,
        name="prompt_layer",
    )(x_flat, x_flat, p_prompt[0].reshape(n_p * seq, PLE), *consts)
    y_p = y_p.reshape(n_p, seq, D)
    new_a_p = tail_a[:, HIST_A - (K_A - 1):, :][None]
    new_b_p = tail_b[:, HIST_B - (K_B - 1):, :][None]

    ns = 16
    assert n_s % ns == 0 and ns % SUBLANES == 0
    n_tiles_s = n_s // ns

    def front_s(i):
        return (0, jnp.minimum(i, n_tiles_s - 1), 0)

    def back_s(i):
        return (0, jnp.maximum(i - 1, 0), 0)

    def t_major(a):
        return jnp.transpose(a, (1, 0, 2))

    x_t = t_major(x_sample)
    y_t, new_a_t, new_b_t = pl.pallas_call(
        functools.partial(_sample_kernel, ns=ns, t_len=t_s),
        grid=(n_tiles_s + 1,),
        in_specs=[pl.BlockSpec((t_s, ns, D), front_s),
                  pl.BlockSpec((t_s, ns, D), back_s),
                  pl.BlockSpec((t_s, ns, PLE), back_s),
                  pl.BlockSpec((K_A - 1, ns, D), front_s),
                  pl.BlockSpec((K_B - 1, ns, D), front_s)] + const_specs,
        out_specs=[pl.BlockSpec((t_s, ns, D), back_s),
                   pl.BlockSpec((K_A - 1, ns, D), front_s),
                   pl.BlockSpec((K_B - 1, ns, D), front_s)],
        out_shape=[jax.ShapeDtypeStruct((t_s, n_s, D), F32),
                   jax.ShapeDtypeStruct((K_A - 1, n_s, D), F32),
                   jax.ShapeDtypeStruct((K_B - 1, n_s, D), F32)],
        scratch_shapes=_common_scratch(ns * t_s) + [
            pltpu.VMEM((ns * t_s, D), F32),
            pltpu.VMEM((ns * t_s, D), F32)],
        compiler_params=params,
        name="sample_layer",
    )(x_t, x_t, t_major(p_sample[0]), t_major(state_conv_a[0]), t_major(state_conv_b[0]),
      *consts)
    return (y_p, t_major(y_t), new_a_p, new_b_p, t_major(new_a_t)[None], t_major(new_b_t)[None])
```

```python
import functools

import jax
import jax.numpy as jnp
from jax import lax
from jax.experimental import pallas as pl
from jax.experimental.pallas import tpu as pltpu

D = 1024
N_PROJ = 9
PLE = 256
K_A = 3
K_B = 31
EPS = 1e-6
LN_EPS = 1e-5
LANES = 128
SUBLANES = 8
N_SLAB = D // LANES
HIST_A = SUBLANES
HIST_B = -(-(K_B - 1) // SUBLANES) * SUBLANES
CONV_ROWS = 64
VMEM_LIMIT_BYTES = 63 * 1024 * 1024

P_AB, P_AC, P_AH, P_AZ, P_BV, P_BG, P_BZ, P_GA, P_GB = range(N_PROJ)
PAIRS = ((P_BV, P_BG), (P_AC, P_AH), (P_AB, P_AZ))
SINGLES = (P_BZ, P_GA, P_GB)
W_IN_ORDER = (tuple((j, c) for pair in PAIRS for c in range(D // 128) for j in pair)
              + tuple((j, c) for j in SINGLES for c in range(D // 128)))

F32 = jnp.float32
BF16 = jnp.bfloat16


def _sigmoid(x):
    return 0.5 * jnp.tanh(0.5 * x) + 0.5


def _silu(x):
    h = 0.5 * x
    return h * jnp.tanh(h) + h


def _rmsnorm(x, g):
    ms = jnp.mean(x * x, axis=-1, keepdims=True)
    return x * lax.rsqrt(ms + EPS) * g


def _layernorm(x, g, b):
    mu = jnp.mean(x, axis=-1, keepdims=True)
    xc = x - mu
    var = jnp.mean(xc * xc, axis=-1, keepdims=True)
    return xc * lax.rsqrt(var + LN_EPS) * g + b


def _lanes(c):
    return slice(c * LANES, (c + 1) * LANES)


def _pipelined_step(load_xf, load_xb, load_pb, store_y, weights, vecs, bufs, convs,
                    do_front=True):
    w_in, w_out_a, w_out_b, w_o, w_pe, w_pg = weights
    g_mix, b_conv_b, ln_g, ln_b, g_ple, g_final = vecs
    u_ref, cb_buf, sbz_ref, gaya_ref, sgb_ref, ybpre_ref, yapre_ref = bufs

    def dot(a, w):
        return jnp.dot(a, w, preferred_element_type=F32)

    def proj(j):
        col = (2 * len(PAIRS) + SINGLES.index(j)) * D
        return dot(u_ref[...], w_in[:, col:col + D])

    def proj_pair(j1, j2):
        col = PAIRS.index((j1, j2)) * 2 * D
        r = dot(u_ref[...], w_in[:, col:col + 2 * D])
        return [(r[:, 2 * c * LANES:(2 * c + 1) * LANES],
                 r[:, (2 * c + 1) * LANES:(2 * c + 2) * LANES]) for c in range(N_SLAB)]

    if do_front:
        u_ref[...] = _rmsnorm(load_xf(), g_mix[...]).astype(BF16)

    cb = cb_buf[...] + b_conv_b[...]
    yb_pre = _silu(_layernorm(cb, ln_g[...], ln_b[...])) * sbz_ref[...]
    ybpre_ref[...] = yb_pre.astype(BF16)

    if do_front:
        convs.put_v([bv * _sigmoid(bg) for bv, bg in proj_pair(P_BV, P_BG)])
        convs.put_s([ac * ah for ac, ah in proj_pair(P_AC, P_AH)])
        convs.conv_a()
        for c, (ab, az) in enumerate(proj_pair(P_AB, P_AZ)):
            yapre_ref[:, _lanes(c)] = (ab * cb_buf[:, _lanes(c)] * _silu(az)).astype(BF16)
        ya = dot(yapre_ref[...], w_out_a[...])

    yb = dot(ybpre_ref[...], w_out_b[...])
    m = gaya_ref[...] + sgb_ref[...] * yb
    h = load_xb() + dot(m.astype(BF16), w_o[...])
    pe = dot(load_pb().astype(BF16), w_pe[...])
    r = _rmsnorm(h, g_ple[...]).astype(BF16)
    pg = _sigmoid(dot(r, w_pg[...]))
    h = h + pg * pe
    store_y(_rmsnorm(h, g_final[...]))

    if do_front:
        convs.conv_b(range(0, N_SLAB // 2))
        sbz_ref[...] = _silu(proj(P_BZ))
        gaya_ref[...] = _sigmoid(proj(P_GA)) * ya
        sgb_ref[...] = _sigmoid(proj(P_GB))
        convs.conv_b(range(N_SLAB // 2, N_SLAB))


def _run_steps(n_tiles, step):
    i = pl.program_id(0)

    @pl.when(i < n_tiles)
    def _():
        step()

    @pl.when(i == n_tiles)
    def _():
        step(do_front=False)


def _w_in_copy(w_hbm, w_vmem, sems, pos):
    j, c = W_IN_ORDER[pos]
    return pltpu.make_async_copy(w_hbm.at[:, pl.ds(j * D + c * LANES, LANES)],
                                 w_vmem.at[:, pl.ds(pos * LANES, LANES)], sems.at[pos])


def _first_step_setup(zero_refs, w_hbm, w_vmem, sems):
    @pl.when(pl.program_id(0) == 0)
    def _():
        for pos in range(len(W_IN_ORDER)):
            _w_in_copy(w_hbm, w_vmem, sems, pos).start()
        for ref in zero_refs:
            ref[...] = jnp.zeros(ref.shape, ref.dtype)
        for pos in range(len(W_IN_ORDER)):
            _w_in_copy(w_hbm, w_vmem, sems, pos).wait()


def _common_scratch(tm):
    return [pltpu.VMEM((tm, D), BF16),
            pltpu.VMEM((tm, D), F32),
            pltpu.VMEM((tm, D), F32),
            pltpu.VMEM((tm, D), F32),
            pltpu.VMEM((tm, D), F32),
            pltpu.VMEM((tm, D), BF16),
            pltpu.VMEM((tm, D), BF16),
            pltpu.VMEM((D, N_PROJ * D), BF16),
            pltpu.SemaphoreType.DMA((len(W_IN_ORDER),))]


N_BUFS = 7
N_COMMON = 9
N_WEIGHTS = 6
N_VECS = 6


def _split_consts(refs):
    a, b, c = N_WEIGHTS, N_WEIGHTS + 2, N_WEIGHTS + 2 + N_VECS
    return refs[:a], refs[a:b], refs[b:c], refs[c:]


class _PromptConvs:
    def __init__(self, slab_a, slab_b, cb_buf, w_conv_a, w_conv_b, tm, t):
        self.slab_a, self.slab_b, self.cb_buf = slab_a, slab_b, cb_buf
        self.w_conv_a, self.w_conv_b, self.tm, self.t = w_conv_a, w_conv_b, tm, t

    def _put(self, slab, hist, pieces):
        prev = slab[:, self.tm:self.tm + hist, :]
        slab[:, 0:hist, :] = jnp.where(self.t == 0, 0.0, prev)
        for c, piece in enumerate(pieces):
            slab[c, hist:hist + self.tm, :] = piece

    def put_v(self, v):
        self._put(self.slab_b, HIST_B, v)

    def put_s(self, s):
        self._put(self.slab_a, HIST_A, s)

    def _conv(self, slab, w_ref, hist, k_taps, slabs=range(N_SLAB)):
        first = hist - (k_taps - 1)
        for c in slabs:
            for r0 in range(0, self.tm, CONV_ROWS):
                acc = None
                for k in range(k_taps):
                    start = first + r0 + k
                    term = slab[c, start:start + CONV_ROWS, :] * w_ref[k:k + 1, _lanes(c)]
                    acc = term if acc is None else acc + term
                self.cb_buf[r0:r0 + CONV_ROWS, _lanes(c)] = acc

    def conv_a(self):
        self._conv(self.slab_a, self.w_conv_a, HIST_A, K_A)

    def conv_b(self, slabs=range(N_SLAB)):
        self._conv(self.slab_b, self.w_conv_b, HIST_B, K_B, slabs)


def _prompt_kernel(*refs, tm, tiles_per_seq, n_tiles):
    n_in = 3
    xf_ref, xb_ref, pb_ref = refs[:n_in]
    weights, (w_conv_a, w_conv_b), vecs, rest = _split_consts(refs[n_in:])
    y_ref, tail_a_ref, tail_b_ref = rest[:3]
    bufs, w_in_vmem, w_in_sems = rest[3:3 + N_BUFS], rest[3 + N_BUFS], rest[4 + N_BUFS]
    slab_a, slab_b = rest[3 + N_COMMON:]
    cb_buf, sbz_ref, gaya_ref, sgb_ref = bufs[1:5]

    tile = jnp.minimum(pl.program_id(0), n_tiles - 1)
    t = tile % tiles_per_seq
    _first_step_setup((slab_a, slab_b, cb_buf, sbz_ref, gaya_ref, sgb_ref),
                      weights[0], w_in_vmem, w_in_sems)

    def store_y(val):
        y_ref[...] = val

    def step(**halves):
        convs = _PromptConvs(slab_a, slab_b, cb_buf, w_conv_a, w_conv_b, tm, t)
        _pipelined_step(lambda: xf_ref[...], lambda: xb_ref[...], lambda: pb_ref[...], store_y,
                        (w_in_vmem,) + tuple(weights[1:]), vecs, bufs, convs, **halves)

    _run_steps(n_tiles, step)

    @pl.when(t == tiles_per_seq - 1)
    def _():
        for c in range(N_SLAB):
            tail_a_ref[:, _lanes(c)] = slab_a[c, tm:tm + HIST_A, :]
            tail_b_ref[:, _lanes(c)] = slab_b[c, tm:tm + HIST_B, :]


class _SampleConvs:
    def __init__(self, st_a_ref, st_b_ref, s_ref, v_ref, cb_buf, w_conv_a, w_conv_b, ns, t_len):
        self.st_a_ref, self.st_b_ref, self.s_ref, self.v_ref = st_a_ref, st_b_ref, s_ref, v_ref
        self.cb_buf = cb_buf
        self.w_conv_a, self.w_conv_b, self.ns, self.t_len = w_conv_a, w_conv_b, ns, t_len

    def put_v(self, pieces):
        for c, piece in enumerate(pieces):
            self.v_ref[:, _lanes(c)] = piece

    def put_s(self, pieces):
        for c, piece in enumerate(pieces):
            self.s_ref[:, _lanes(c)] = piece

    def _conv(self, st_ref, new_ref, w_ref, k_taps, slabs=range(N_SLAB)):
        ns, t_len, hist = self.ns, self.t_len, k_taps - 1
        for c in slabs:
            w = [w_ref[k:k + 1, _lanes(c)] for k in range(k_taps)]
            for n0 in range(0, ns, SUBLANES):
                acc = [None] * t_len
                for e in range(hist + t_len):
                    if e < hist:
                        x = st_ref[e, n0:n0 + SUBLANES, _lanes(c)]
                    else:
                        r = (e - hist) * ns + n0
                        x = new_ref[r:r + SUBLANES, _lanes(c)]
                    for t in range(t_len):
                        k = e - t
                        if 0 <= k < k_taps:
                            term = x * w[k]
                            acc[t] = term if acc[t] is None else acc[t] + term
                for t in range(t_len):
                    self.cb_buf[t * ns + n0:t * ns + n0 + SUBLANES, _lanes(c)] = acc[t]

    def conv_a(self):
        self._conv(self.st_a_ref, self.s_ref, self.w_conv_a, K_A)

    def conv_b(self, slabs=range(N_SLAB)):
        self._conv(self.st_b_ref, self.v_ref, self.w_conv_b, K_B, slabs)


def _sample_kernel(*refs, ns, t_len, n_tiles):
    n_in = 5
    xf_ref, xb_ref, pb_ref, st_a_ref, st_b_ref = refs[:n_in]
    weights, (w_conv_a, w_conv_b), vecs, rest = _split_consts(refs[n_in:])
    y_ref, new_a_ref, new_b_ref = rest[:3]
    bufs, w_in_vmem, w_in_sems = rest[3:3 + N_BUFS], rest[3 + N_BUFS], rest[4 + N_BUFS]
    s_ref, v_ref = rest[3 + N_COMMON:]
    cb_buf, sbz_ref, gaya_ref, sgb_ref = bufs[1:5]
    rows = ns * t_len

    _first_step_setup((cb_buf, sbz_ref, gaya_ref, sgb_ref), weights[0], w_in_vmem, w_in_sems)

    def store_y(val):
        y_ref[...] = val.reshape(t_len, ns, D)

    def step(**halves):
        convs = _SampleConvs(st_a_ref, st_b_ref, s_ref, v_ref, cb_buf, w_conv_a, w_conv_b,
                             ns, t_len)
        _pipelined_step(lambda: xf_ref[...].reshape(rows, D),
                        lambda: xb_ref[...].reshape(rows, D),
                        lambda: pb_ref[...].reshape(rows, PLE), store_y,
                        (w_in_vmem,) + tuple(weights[1:]), vecs, bufs, convs, **halves)

    _run_steps(n_tiles, step)

    for e in range(K_A - 1):
        src = e + t_len - (K_A - 1)
        new_a_ref[e] = s_ref[src * ns:(src + 1) * ns, :]
    for e in range(K_B - 1):
        src = e + t_len
        if src < K_B - 1:
            new_b_ref[e] = st_b_ref[src]
        else:
            src -= K_B - 1
            new_b_ref[e] = v_ref[src * ns:(src + 1) * ns, :]


def _whole(memory_space=pltpu.VMEM):
    return pl.BlockSpec(memory_space=memory_space)


def kernel(x_prompt, x_sample, state_conv_a, state_conv_b, p_prompt, p_sample, g_mix, w_in, w_conv_a, w_out_a, w_conv_b, b_conv_b, ln_g, ln_b, w_out_b, w_o, w_pe, g_ple, w_pg, g_final):
    depth = w_in.shape[0]
    assert depth == 1, "single-layer trunk"
    n_p, seq, d = x_prompt.shape
    n_s, t_s, _ = x_sample.shape
    assert d == D and w_in.shape[2] == N_PROJ * D
    assert t_s >= K_A - 1 and t_s <= K_B - 1

    consts = (w_in[0].astype(BF16),
              w_out_a[0].astype(BF16), w_out_b[0].astype(BF16),
              w_o[0].astype(BF16), w_pe[0].astype(BF16), w_pg[0].astype(BF16),
              w_conv_a[0], w_conv_b[0],
              g_mix[0][None], b_conv_b[0][None], ln_g[0][None], ln_b[0][None],
              g_ple[0][None], g_final[None])
    const_specs = [_whole(pl.ANY)] + [_whole() for _ in consts[1:]]
    params = pltpu.CompilerParams(dimension_semantics=("arbitrary",),
                                  vmem_limit_bytes=VMEM_LIMIT_BYTES)

    tm = 512
    assert seq % tm == 0 and tm >= HIST_B and tm % CONV_ROWS == 0
    tiles_per_seq = seq // tm
    n_tiles = n_p * tiles_per_seq

    def front(i):
        return (jnp.minimum(i, n_tiles - 1), 0)

    def back(i):
        return (jnp.maximum(i - 1, 0), 0)

    def seq_of_front(i):
        return (jnp.minimum(i, n_tiles - 1) // tiles_per_seq, 0, 0)

    x_flat = x_prompt.reshape(n_p * seq, D)
    y_p, tail_a, tail_b = pl.pallas_call(
        functools.partial(_prompt_kernel, tm=tm, tiles_per_seq=tiles_per_seq, n_tiles=n_tiles),
        grid=(n_tiles + 1,),
        in_specs=[pl.BlockSpec((tm, D), front),
                  pl.BlockSpec((tm, D), back),
                  pl.BlockSpec((tm, PLE), back)] + const_specs,
        out_specs=[pl.BlockSpec((tm, D), back),
                   pl.BlockSpec((None, HIST_A, D), seq_of_front),
                   pl.BlockSpec((None, HIST_B, D), seq_of_front)],
        out_shape=[jax.ShapeDtypeStruct((n_p * seq, D), F32),
                   jax.ShapeDtypeStruct((n_p, HIST_A, D), F32),
                   jax.ShapeDtypeStruct((n_p, HIST_B, D), F32)],
        scratch_shapes=_common_scratch(tm) + [
            pltpu.VMEM((N_SLAB, HIST_A + tm, LANES), F32),
            pltpu.VMEM((N_SLAB, HIST_B + tm, LANES), F32)],
        compiler_params=params,
        name="prompt_layer",
    )(x_flat, x_flat, p_prompt[0].reshape(n_p * seq, PLE), *consts)
    y_p = y_p.reshape(n_p, seq, D)
    new_a_p = tail_a[:, HIST_A - (K_A - 1):, :][None]
    new_b_p = tail_b[:, HIST_B - (K_B - 1):, :][None]

    ns = 16
    assert n_s % ns == 0 and ns % SUBLANES == 0
    n_tiles_s = n_s // ns

    def front_s(i):
        return (0, jnp.minimum(i, n_tiles_s - 1), 0)

    def back_s(i):
        return (0, jnp.maximum(i - 1, 0), 0)

    def t_major(a):
        return jnp.transpose(a, (1, 0, 2))

    x_t = t_major(x_sample)
    y_t, new_a_t, new_b_t = pl.pallas_call(
        functools.partial(_sample_kernel, ns=ns, t_len=t_s, n_tiles=n_tiles_s),
        grid=(n_tiles_s + 1,),
        in_specs=[pl.BlockSpec((t_s, ns, D), front_s),
                  pl.BlockSpec((t_s, ns, D), back_s),
                  pl.BlockSpec((t_s, ns, PLE), back_s),
                  pl.BlockSpec((K_A - 1, ns, D), front_s),
                  pl.BlockSpec((K_B - 1, ns, D), front_s)] + const_specs,
        out_specs=[pl.BlockSpec((t_s, ns, D), back_s),
                   pl.BlockSpec((K_A - 1, ns, D), front_s),
                   pl.BlockSpec((K_B - 1, ns, D), front_s)],
        out_shape=[jax.ShapeDtypeStruct((t_s, n_s, D), F32),
                   jax.ShapeDtypeStruct((K_A - 1, n_s, D), F32),
                   jax.ShapeDtypeStruct((K_B - 1, n_s, D), F32)],
        scratch_shapes=_common_scratch(ns * t_s) + [
            pltpu.VMEM((ns * t_s, D), F32),
            pltpu.VMEM((ns * t_s, D), F32)],
        compiler_params=params,
        name="sample_layer",
    )(x_t, x_t, t_major(p_sample[0]), t_major(state_conv_a[0]), t_major(state_conv_b[0]),
      *consts)
    return (y_p, t_major(y_t), new_a_p, new_b_p, t_major(new_a_t)[None], t_major(new_b_t)[None])
```

```python
import functools

import jax
import jax.numpy as jnp
from jax import lax
from jax.experimental import pallas as pl
from jax.experimental.pallas import tpu as pltpu

D = 1024
N_PROJ = 9
PLE = 256
K_A = 3
K_B = 31
EPS = 1e-6
LN_EPS = 1e-5
LANES = 128
SUBLANES = 8
N_SLAB = D // LANES
HIST_A = SUBLANES
HIST_B = -(-(K_B - 1) // SUBLANES) * SUBLANES
CONV_ROWS = 64
STAGE_ROWS = 128
N_SEMS = 16
SEM_WHOLE = N_SEMS - 1
VMEM_LIMIT_BYTES = 63 * 1024 * 1024

P_AB, P_AC, P_AH, P_AZ, P_BV, P_BG, P_BZ, P_GA, P_GB = range(N_PROJ)
PAIRS = ((P_BV, P_BG), (P_AC, P_AH), (P_AB, P_AZ))
SINGLES = (P_BZ, P_GA, P_GB)
W_IN_ORDER = (tuple((j, c) for pair in PAIRS for c in range(D // 128) for j in pair)
              + tuple((j, c) for j in SINGLES for c in range(D // 128)))

F32 = jnp.float32
BF16 = jnp.bfloat16


def _sigmoid(x):
    return 0.5 * jnp.tanh(0.5 * x) + 0.5


def _silu(x):
    h = 0.5 * x
    return h * jnp.tanh(h) + h


def _rmsnorm(x, g):
    ms = jnp.mean(x * x, axis=-1, keepdims=True)
    return x * lax.rsqrt(ms + EPS) * g


def _layernorm(x, g, b):
    mu = jnp.mean(x, axis=-1, keepdims=True)
    xc = x - mu
    var = jnp.mean(xc * xc, axis=-1, keepdims=True)
    return xc * lax.rsqrt(var + LN_EPS) * g + b


def _lanes(c):
    return slice(c * LANES, (c + 1) * LANES)


def _pipelined_step(load_xf, load_xb, load_pb, store_y, weights, vecs, bufs, convs,
                    do_front=True):
    w_in, w_out_a, w_out_b, w_o, w_pe, w_pg = weights
    g_mix, b_conv_b, ln_g, ln_b, g_ple, g_final = vecs
    u_ref, cb_buf, sbz_ref, gaya_ref, sgb_ref, ybpre_ref, yapre_ref = bufs

    def dot(a, w):
        return jnp.dot(a, w, preferred_element_type=F32)

    def proj(j):
        col = (2 * len(PAIRS) + SINGLES.index(j)) * D
        return dot(u_ref[...], w_in[:, col:col + D])

    def proj_pair(j1, j2):
        col = PAIRS.index((j1, j2)) * 2 * D
        r = dot(u_ref[...], w_in[:, col:col + 2 * D])
        return [(r[:, 2 * c * LANES:(2 * c + 1) * LANES],
                 r[:, (2 * c + 1) * LANES:(2 * c + 2) * LANES]) for c in range(N_SLAB)]

    if do_front:
        u_ref[...] = _rmsnorm(load_xf(), g_mix[...]).astype(BF16)

    cb = cb_buf[...] + b_conv_b[...]
    yb_pre = _silu(_layernorm(cb, ln_g[...], ln_b[...])) * sbz_ref[...]
    ybpre_ref[...] = yb_pre.astype(BF16)

    if do_front:
        convs.put_v([bv * _sigmoid(bg) for bv, bg in proj_pair(P_BV, P_BG)])
        convs.put_s([ac * ah for ac, ah in proj_pair(P_AC, P_AH)])
        convs.conv_a()
        for c, (ab, az) in enumerate(proj_pair(P_AB, P_AZ)):
            yapre_ref[:, _lanes(c)] = (ab * cb_buf[:, _lanes(c)] * _silu(az)).astype(BF16)
        ya = dot(yapre_ref[...], w_out_a[...])

    yb = dot(ybpre_ref[...], w_out_b[...])
    m = gaya_ref[...] + sgb_ref[...] * yb
    h = load_xb() + dot(m.astype(BF16), w_o[...])
    pe = dot(load_pb().astype(BF16), w_pe[...])
    r = _rmsnorm(h, g_ple[...]).astype(BF16)
    pg = _sigmoid(dot(r, w_pg[...]))
    h = h + pg * pe
    store_y(_rmsnorm(h, g_final[...]))

    if do_front:
        convs.conv_b(range(0, N_SLAB // 2))
        sbz_ref[...] = _silu(proj(P_BZ))
        gaya_ref[...] = _sigmoid(proj(P_GA)) * ya
        sgb_ref[...] = _sigmoid(proj(P_GB))
        convs.conv_b(range(N_SLAB // 2, N_SLAB))


def _run_steps(n_tiles, step):
    i = pl.program_id(0)

    @pl.when(i < n_tiles)
    def _():
        step()

    @pl.when(i == n_tiles)
    def _():
        step(do_front=False)


def _convert_w_in(w_f32, w_vmem, stage_refs, sems):
    slots = [(ref, r0) for ref in stage_refs for r0 in range(0, ref.shape[0], STAGE_ROWS)]
    chunks = [(j, r0) for j in range(N_PROJ) for r0 in range(0, D, STAGE_ROWS)]
    pos_of = {jc: pos for pos, jc in enumerate(W_IN_ORDER)}

    def copy(k):
        j, r0 = chunks[k]
        ref, s0 = slots[k % len(slots)]
        return pltpu.make_async_copy(w_f32.at[pl.ds(r0, STAGE_ROWS), pl.ds(j * D, D)],
                                     ref.at[pl.ds(s0, STAGE_ROWS), :], sems.at[k % len(slots)])

    for k in range(len(slots)):
        copy(k).start()
    for k, (j, r0) in enumerate(chunks):
        copy(k).wait()
        ref, s0 = slots[k % len(slots)]
        for c in range(N_SLAB):
            w_vmem[r0:r0 + STAGE_ROWS, _lanes(pos_of[(j, c)])] = (
                ref[s0:s0 + STAGE_ROWS, _lanes(c)].astype(BF16))
        if k + len(slots) < len(chunks):
            copy(k + len(slots)).start()


def _zero(refs):
    for ref in refs:
        ref[...] = jnp.zeros(ref.shape, ref.dtype)


def _common_scratch(tm):
    return [pltpu.VMEM((tm, D), BF16),
            pltpu.VMEM((tm, D), F32),
            pltpu.VMEM((tm, D), F32),
            pltpu.VMEM((tm, D), F32),
            pltpu.VMEM((tm, D), F32),
            pltpu.VMEM((tm, D), BF16),
            pltpu.VMEM((tm, D), BF16),
            pltpu.VMEM((D, N_PROJ * D), BF16),
            pltpu.SemaphoreType.DMA((N_SEMS,))]


N_BUFS = 7
N_COMMON = 9
N_WEIGHTS = 6
N_VECS = 6


def _split_consts(refs):
    a, b, c = N_WEIGHTS, N_WEIGHTS + 2, N_WEIGHTS + 2 + N_VECS
    return refs[:a], refs[a:b], refs[b:c], refs[c:]


class _PromptConvs:
    def __init__(self, slab_a, slab_b, cb_buf, w_conv_a, w_conv_b, tm, t):
        self.slab_a, self.slab_b, self.cb_buf = slab_a, slab_b, cb_buf
        self.w_conv_a, self.w_conv_b, self.tm, self.t = w_conv_a, w_conv_b, tm, t

    def _put(self, slab, hist, pieces):
        prev = slab[:, self.tm:self.tm + hist, :]
        slab[:, 0:hist, :] = jnp.where(self.t == 0, 0.0, prev)
        for c, piece in enumerate(pieces):
            slab[c, hist:hist + self.tm, :] = piece

    def put_v(self, v):
        self._put(self.slab_b, HIST_B, v)

    def put_s(self, s):
        self._put(self.slab_a, HIST_A, s)

    def _conv(self, slab, w_ref, hist, k_taps, slabs=range(N_SLAB)):
        first = hist - (k_taps - 1)
        for c in slabs:
            for r0 in range(0, self.tm, CONV_ROWS):
                acc = None
                for k in range(k_taps):
                    start = first + r0 + k
                    term = slab[c, start:start + CONV_ROWS, :] * w_ref[k:k + 1, _lanes(c)]
                    acc = term if acc is None else acc + term
                self.cb_buf[r0:r0 + CONV_ROWS, _lanes(c)] = acc

    def conv_a(self):
        self._conv(self.slab_a, self.w_conv_a, HIST_A, K_A)

    def conv_b(self, slabs=range(N_SLAB)):
        self._conv(self.slab_b, self.w_conv_b, HIST_B, K_B, slabs)


def _prompt_kernel(*refs, tm, tiles_per_seq, n_tiles):
    n_in = 3
    xf_ref, xb_ref, pb_ref = refs[:n_in]
    weights, (w_conv_a, w_conv_b), vecs, rest = _split_consts(refs[n_in:])
    y_ref, tail_a_ref, tail_b_ref, w_in_bf16_hbm = rest[:4]
    bufs, w_in_vmem, sems = rest[4:4 + N_BUFS], rest[4 + N_BUFS], rest[5 + N_BUFS]
    slab_a, slab_b = rest[4 + N_COMMON:]
    cb_buf, sbz_ref, gaya_ref, sgb_ref = bufs[1:5]

    tile = jnp.minimum(pl.program_id(0), n_tiles - 1)
    t = tile % tiles_per_seq
    write_back = pltpu.make_async_copy(w_in_vmem, w_in_bf16_hbm, sems.at[SEM_WHOLE])

    @pl.when(pl.program_id(0) == 0)
    def _():
        stage = (sbz_ref, gaya_ref, sgb_ref)
        assert sum(r.shape[0] // STAGE_ROWS for r in stage) < SEM_WHOLE
        _convert_w_in(weights[0], w_in_vmem, stage, sems)
        _zero((slab_a, slab_b, cb_buf, sbz_ref, gaya_ref, sgb_ref))
        write_back.start()

    def store_y(val):
        y_ref[...] = val

    def step(**halves):
        convs = _PromptConvs(slab_a, slab_b, cb_buf, w_conv_a, w_conv_b, tm, t)
        _pipelined_step(lambda: xf_ref[...], lambda: xb_ref[...], lambda: pb_ref[...], store_y,
                        (w_in_vmem,) + tuple(weights[1:]), vecs, bufs, convs, **halves)

    _run_steps(n_tiles, step)

    @pl.when(pl.program_id(0) == n_tiles)
    def _():
        write_back.wait()

    @pl.when(t == tiles_per_seq - 1)
    def _():
        for c in range(N_SLAB):
            tail_a_ref[:, _lanes(c)] = slab_a[c, tm:tm + HIST_A, :]
            tail_b_ref[:, _lanes(c)] = slab_b[c, tm:tm + HIST_B, :]


class _SampleConvs:
    def __init__(self, st_a_ref, st_b_ref, s_ref, v_ref, cb_buf, w_conv_a, w_conv_b, ns, t_len):
        self.st_a_ref, self.st_b_ref, self.s_ref, self.v_ref = st_a_ref, st_b_ref, s_ref, v_ref
        self.cb_buf = cb_buf
        self.w_conv_a, self.w_conv_b, self.ns, self.t_len = w_conv_a, w_conv_b, ns, t_len

    def put_v(self, pieces):
        for c, piece in enumerate(pieces):
            self.v_ref[:, _lanes(c)] = piece

    def put_s(self, pieces):
        for c, piece in enumerate(pieces):
            self.s_ref[:, _lanes(c)] = piece

    def _conv(self, st_ref, new_ref, w_ref, k_taps, slabs=range(N_SLAB)):
        ns, t_len, hist = self.ns, self.t_len, k_taps - 1
        for c in slabs:
            w = [w_ref[k:k + 1, _lanes(c)] for k in range(k_taps)]
            for n0 in range(0, ns, SUBLANES):
                acc = [None] * t_len
                for e in range(hist + t_len):
                    if e < hist:
                        x = st_ref[e, n0:n0 + SUBLANES, _lanes(c)]
                    else:
                        r = (e - hist) * ns + n0
                        x = new_ref[r:r + SUBLANES, _lanes(c)]
                    for t in range(t_len):
                        k = e - t
                        if 0 <= k < k_taps:
                            term = x * w[k]
                            acc[t] = term if acc[t] is None else acc[t] + term
                for t in range(t_len):
                    self.cb_buf[t * ns + n0:t * ns + n0 + SUBLANES, _lanes(c)] = acc[t]

    def conv_a(self):
        self._conv(self.st_a_ref, self.s_ref, self.w_conv_a, K_A)

    def conv_b(self, slabs=range(N_SLAB)):
        self._conv(self.st_b_ref, self.v_ref, self.w_conv_b, K_B, slabs)


def _sample_kernel(*refs, ns, t_len, n_tiles):
    n_in = 5
    xf_ref, xb_ref, pb_ref, st_a_ref, st_b_ref = refs[:n_in]
    weights, (w_conv_a, w_conv_b), vecs, rest = _split_consts(refs[n_in:])
    y_ref, new_a_ref, new_b_ref = rest[:3]
    bufs, w_in_vmem, sems = rest[3:3 + N_BUFS], rest[3 + N_BUFS], rest[4 + N_BUFS]
    s_ref, v_ref = rest[3 + N_COMMON:]
    cb_buf, sbz_ref, gaya_ref, sgb_ref = bufs[1:5]
    rows = ns * t_len

    @pl.when(pl.program_id(0) == 0)
    def _():
        reload = pltpu.make_async_copy(weights[0], w_in_vmem, sems.at[SEM_WHOLE])
        reload.start()
        _zero((cb_buf, sbz_ref, gaya_ref, sgb_ref))
        reload.wait()

    def store_y(val):
        y_ref[...] = val.reshape(t_len, ns, D)

    def step(**halves):
        convs = _SampleConvs(st_a_ref, st_b_ref, s_ref, v_ref, cb_buf, w_conv_a, w_conv_b,
                             ns, t_len)
        _pipelined_step(lambda: xf_ref[...].reshape(rows, D),
                        lambda: xb_ref[...].reshape(rows, D),
                        lambda: pb_ref[...].reshape(rows, PLE), store_y,
                        (w_in_vmem,) + tuple(weights[1:]), vecs, bufs, convs, **halves)

    _run_steps(n_tiles, step)

    for e in range(K_A - 1):
        src = e + t_len - (K_A - 1)
        new_a_ref[e] = s_ref[src * ns:(src + 1) * ns, :]
    for e in range(K_B - 1):
        src = e + t_len
        if src < K_B - 1:
            new_b_ref[e] = st_b_ref[src]
        else:
            src -= K_B - 1
            new_b_ref[e] = v_ref[src * ns:(src + 1) * ns, :]


def _whole(memory_space=pltpu.VMEM):
    return pl.BlockSpec(memory_space=memory_space)


def kernel(x_prompt, x_sample, state_conv_a, state_conv_b, p_prompt, p_sample, g_mix, w_in, w_conv_a, w_out_a, w_conv_b, b_conv_b, ln_g, ln_b, w_out_b, w_o, w_pe, g_ple, w_pg, g_final):
    depth = w_in.shape[0]
    assert depth == 1, "single-layer trunk"
    n_p, seq, d = x_prompt.shape
    n_s, t_s, _ = x_sample.shape
    assert d == D and w_in.shape[2] == N_PROJ * D
    assert t_s >= K_A - 1 and t_s <= K_B - 1

    consts = (w_in[0],
              w_out_a[0].astype(BF16), w_out_b[0].astype(BF16),
              w_o[0].astype(BF16), w_pe[0].astype(BF16), w_pg[0].astype(BF16),
              w_conv_a[0], w_conv_b[0],
              g_mix[0][None], b_conv_b[0][None], ln_g[0][None], ln_b[0][None],
              g_ple[0][None], g_final[None])
    const_specs = [_whole(pl.ANY)] + [_whole() for _ in consts[1:]]
    params = pltpu.CompilerParams(dimension_semantics=("arbitrary",),
                                  vmem_limit_bytes=VMEM_LIMIT_BYTES)

    tm = 512
    assert seq % tm == 0 and tm >= HIST_B and tm % CONV_ROWS == 0
    tiles_per_seq = seq // tm
    n_tiles = n_p * tiles_per_seq

    def front(i):
        return (jnp.minimum(i, n_tiles - 1), 0)

    def back(i):
        return (jnp.maximum(i - 1, 0), 0)

    def seq_of_front(i):
        return (jnp.minimum(i, n_tiles - 1) // tiles_per_seq, 0, 0)

    x_flat = x_prompt.reshape(n_p * seq, D)
    y_p, tail_a, tail_b, w_in_bf16 = pl.pallas_call(
        functools.partial(_prompt_kernel, tm=tm, tiles_per_seq=tiles_per_seq, n_tiles=n_tiles),
        grid=(n_tiles + 1,),
        in_specs=[pl.BlockSpec((tm, D), front),
                  pl.BlockSpec((tm, D), back),
                  pl.BlockSpec((tm, PLE), back)] + const_specs,
        out_specs=[pl.BlockSpec((tm, D), back),
                   pl.BlockSpec((None, HIST_A, D), seq_of_front),
                   pl.BlockSpec((None, HIST_B, D), seq_of_front),
                   _whole(pl.ANY)],
        out_shape=[jax.ShapeDtypeStruct((n_p * seq, D), F32),
                   jax.ShapeDtypeStruct((n_p, HIST_A, D), F32),
                   jax.ShapeDtypeStruct((n_p, HIST_B, D), F32),
                   jax.ShapeDtypeStruct((D, N_PROJ * D), BF16)],
        scratch_shapes=_common_scratch(tm) + [
            pltpu.VMEM((N_SLAB, HIST_A + tm, LANES), F32),
            pltpu.VMEM((N_SLAB, HIST_B + tm, LANES), F32)],
        compiler_params=params,
        name="prompt_layer",
    )(x_flat, x_flat, p_prompt[0].reshape(n_p * seq, PLE), *consts)
    y_p = y_p.reshape(n_p, seq, D)
    new_a_p = tail_a[:, HIST_A - (K_A - 1):, :][None]
    new_b_p = tail_b[:, HIST_B - (K_B - 1):, :][None]

    ns = 16
    assert n_s % ns == 0 and ns % SUBLANES == 0
    n_tiles_s = n_s // ns

    def front_s(i):
        return (0, jnp.minimum(i, n_tiles_s - 1), 0)

    def back_s(i):
        return (0, jnp.maximum(i - 1, 0), 0)

    def t_major(a):
        return jnp.transpose(a, (1, 0, 2))

    x_t = t_major(x_sample)
    y_t, new_a_t, new_b_t = pl.pallas_call(
        functools.partial(_sample_kernel, ns=ns, t_len=t_s, n_tiles=n_tiles_s),
        grid=(n_tiles_s + 1,),
        in_specs=[pl.BlockSpec((t_s, ns, D), front_s),
                  pl.BlockSpec((t_s, ns, D), back_s),
                  pl.BlockSpec((t_s, ns, PLE), back_s),
                  pl.BlockSpec((K_A - 1, ns, D), front_s),
                  pl.BlockSpec((K_B - 1, ns, D), front_s)] + const_specs,
        out_specs=[pl.BlockSpec((t_s, ns, D), back_s),
                   pl.BlockSpec((K_A - 1, ns, D), front_s),
                   pl.BlockSpec((K_B - 1, ns, D), front_s)],
        out_shape=[jax.ShapeDtypeStruct((t_s, n_s, D), F32),
                   jax.ShapeDtypeStruct((K_A - 1, n_s, D), F32),
                   jax.ShapeDtypeStruct((K_B - 1, n_s, D), F32)],
        scratch_shapes=_common_scratch(ns * t_s) + [
            pltpu.VMEM((ns * t_s, D), F32),
            pltpu.VMEM((ns * t_s, D), F32)],
        compiler_params=params,
        name="sample_layer",
    )(x_t, x_t, t_major(p_sample[0]), t_major(state_conv_a[0]), t_major(state_conv_b[0]),
      w_in_bf16, *consts[1:])
    return (y_p, t_major(y_t), new_a_p, new_b_p, t_major(new_a_t)[None], t_major(new_b_t)[None])
```

```python
import functools

import jax
import jax.numpy as jnp
from jax import lax
from jax.experimental import pallas as pl
from jax.experimental.pallas import tpu as pltpu

D = 1024
N_PROJ = 9
PLE = 256
K_A = 3
K_B = 31
EPS = 1e-6
LN_EPS = 1e-5
LANES = 128
SUBLANES = 8
N_SLAB = D // LANES
HIST_A = SUBLANES
HIST_B = -(-(K_B - 1) // SUBLANES) * SUBLANES
CONV_ROWS = 64
STAGE_ROWS = 128
N_SEMS = 16
SEM_WHOLE = N_SEMS - 1
VMEM_LIMIT_BYTES = 63 * 1024 * 1024

P_AB, P_AC, P_AH, P_AZ, P_BV, P_BG, P_BZ, P_GA, P_GB = range(N_PROJ)
PAIRS = ((P_BV, P_BG), (P_AC, P_AH), (P_AB, P_AZ))
SINGLES = (P_BZ, P_GA, P_GB)
W_IN_ORDER = (tuple((j, c) for pair in PAIRS for c in range(D // 128) for j in pair)
              + tuple((j, c) for j in SINGLES for c in range(D // 128)))

F32 = jnp.float32
BF16 = jnp.bfloat16


def _sigmoid(x):
    return 0.5 * jnp.tanh(0.5 * x) + 0.5


def _silu(x):
    h = 0.5 * x
    return h * jnp.tanh(h) + h


def _rmsnorm(x, g):
    ms = jnp.mean(x * x, axis=-1, keepdims=True)
    return x * lax.rsqrt(ms + EPS) * g


def _layernorm(x, g, b):
    mu = jnp.mean(x, axis=-1, keepdims=True)
    xc = x - mu
    var = jnp.mean(xc * xc, axis=-1, keepdims=True)
    return xc * lax.rsqrt(var + LN_EPS) * g + b


def _lanes(c):
    return slice(c * LANES, (c + 1) * LANES)


def _pipelined_step(load_xf, load_xb, load_pb, store_y, weights, vecs, bufs, convs,
                    do_front=True):
    w_in, w_out_a, w_out_b, w_o, w_pe, w_pg = weights
    g_mix, b_conv_b, ln_g, ln_b, g_ple, g_final = vecs
    u_ref, cb_buf, sbz_ref, gaya_ref, sgb_ref, ybpre_ref, yapre_ref = bufs

    def dot(a, w):
        return jnp.dot(a, w, preferred_element_type=F32)

    def proj(j):
        col = (2 * len(PAIRS) + SINGLES.index(j)) * D
        return dot(u_ref[...], w_in[:, col:col + D])

    def proj_pair(j1, j2):
        col = PAIRS.index((j1, j2)) * 2 * D
        r = dot(u_ref[...], w_in[:, col:col + 2 * D])
        return [(r[:, 2 * c * LANES:(2 * c + 1) * LANES],
                 r[:, (2 * c + 1) * LANES:(2 * c + 2) * LANES]) for c in range(N_SLAB)]

    if do_front:
        u_ref[...] = _rmsnorm(load_xf(), g_mix[...]).astype(BF16)

    cb = cb_buf[...] + b_conv_b[...]
    yb_pre = _silu(_layernorm(cb, ln_g[...], ln_b[...])) * sbz_ref[...]
    ybpre_ref[...] = yb_pre.astype(BF16)

    if do_front:
        convs.put_v([bv * _sigmoid(bg) for bv, bg in proj_pair(P_BV, P_BG)])
        convs.put_s([ac * ah for ac, ah in proj_pair(P_AC, P_AH)])
        convs.conv_a()
        for c, (ab, az) in enumerate(proj_pair(P_AB, P_AZ)):
            yapre_ref[:, _lanes(c)] = (ab * cb_buf[:, _lanes(c)] * _silu(az)).astype(BF16)
        ya = dot(yapre_ref[...], w_out_a[...])

    yb = dot(ybpre_ref[...], w_out_b[...])
    m = gaya_ref[...] + sgb_ref[...] * yb
    h = load_xb() + dot(m.astype(BF16), w_o[...])
    pe = dot(load_pb().astype(BF16), w_pe[...])
    r = _rmsnorm(h, g_ple[...]).astype(BF16)
    pg = _sigmoid(dot(r, w_pg[...]))
    h = h + pg * pe
    store_y(_rmsnorm(h, g_final[...]))

    if do_front:
        convs.conv_b(range(0, N_SLAB // 2))
        sbz_ref[...] = _silu(proj(P_BZ))
        gaya_ref[...] = _sigmoid(proj(P_GA)) * ya
        sgb_ref[...] = _sigmoid(proj(P_GB))
        convs.conv_b(range(N_SLAB // 2, N_SLAB))


def _run_steps(n_tiles, step):
    i = pl.program_id(0)

    @pl.when(i < n_tiles)
    def _():
        step()

    @pl.when(i == n_tiles)
    def _():
        step(do_front=False)


def _convert_w_in(w_f32, w_vmem, stage_refs, sems):
    slots = [(ref, r0) for ref in stage_refs for r0 in range(0, ref.shape[0], STAGE_ROWS)]
    chunks = [(j, r0) for j in range(N_PROJ) for r0 in range(0, D, STAGE_ROWS)]
    pos_of = {jc: pos for pos, jc in enumerate(W_IN_ORDER)}

    def copy(k):
        j, r0 = chunks[k]
        ref, s0 = slots[k % len(slots)]
        return pltpu.make_async_copy(w_f32.at[pl.ds(r0, STAGE_ROWS), pl.ds(j * D, D)],
                                     ref.at[pl.ds(s0, STAGE_ROWS), :], sems.at[k % len(slots)])

    for k in range(len(slots)):
        copy(k).start()
    for k, (j, r0) in enumerate(chunks):
        copy(k).wait()
        ref, s0 = slots[k % len(slots)]
        for c in range(N_SLAB):
            w_vmem[r0:r0 + STAGE_ROWS, _lanes(pos_of[(j, c)])] = (
                ref[s0:s0 + STAGE_ROWS, _lanes(c)].astype(BF16))
        if k + len(slots) < len(chunks):
            copy(k + len(slots)).start()


def _zero(refs):
    for ref in refs:
        ref[...] = jnp.zeros(ref.shape, ref.dtype)


def _common_scratch(tm):
    return [pltpu.VMEM((tm, D), BF16),
            pltpu.VMEM((tm, D), F32),
            pltpu.VMEM((tm, D), F32),
            pltpu.VMEM((tm, D), F32),
            pltpu.VMEM((tm, D), F32),
            pltpu.VMEM((tm, D), BF16),
            pltpu.VMEM((tm, D), BF16),
            pltpu.VMEM((D, N_PROJ * D), BF16),
            pltpu.SemaphoreType.DMA((N_SEMS,))]


N_BUFS = 7
N_COMMON = 9
N_WEIGHTS = 6
N_VECS = 6


def _split_consts(refs):
    a, b, c = N_WEIGHTS, N_WEIGHTS + 2, N_WEIGHTS + 2 + N_VECS
    return refs[:a], refs[a:b], refs[b:c], refs[c:]


class _PromptConvs:
    def __init__(self, slab_a, slab_b, cb_buf, w_conv_a, w_conv_b, tm, t):
        self.slab_a, self.slab_b, self.cb_buf = slab_a, slab_b, cb_buf
        self.w_conv_a, self.w_conv_b, self.tm, self.t = w_conv_a, w_conv_b, tm, t

    def _put(self, slab, hist, pieces):
        prev = slab[:, self.tm:self.tm + hist, :]
        slab[:, 0:hist, :] = jnp.where(self.t == 0, 0.0, prev)
        for c, piece in enumerate(pieces):
            slab[c, hist:hist + self.tm, :] = piece

    def put_v(self, v):
        self._put(self.slab_b, HIST_B, v)

    def put_s(self, s):
        self._put(self.slab_a, HIST_A, s)

    def _conv(self, slab, w_ref, hist, k_taps, slabs=range(N_SLAB)):
        first = hist - (k_taps - 1)
        for c in slabs:
            for r0 in range(0, self.tm, CONV_ROWS):
                acc = None
                for k in range(k_taps):
                    start = first + r0 + k
                    term = slab[c, start:start + CONV_ROWS, :] * w_ref[k:k + 1, _lanes(c)]
                    acc = term if acc is None else acc + term
                self.cb_buf[r0:r0 + CONV_ROWS, _lanes(c)] = acc

    def conv_a(self):
        self._conv(self.slab_a, self.w_conv_a, HIST_A, K_A)

    def conv_b(self, slabs=range(N_SLAB)):
        self._conv(self.slab_b, self.w_conv_b, HIST_B, K_B, slabs)


def _prompt_kernel(*refs, tm, tiles_per_seq, n_tiles):
    n_in = 3
    xf_ref, xb_ref, pb_ref = refs[:n_in]
    weights, (w_conv_a, w_conv_b), vecs, rest = _split_consts(refs[n_in:])
    y_ref, tail_a_ref, tail_b_ref, w_in_bf16_hbm = rest[:4]
    bufs, w_in_vmem, sems = rest[4:4 + N_BUFS], rest[4 + N_BUFS], rest[5 + N_BUFS]
    slab_a, slab_b = rest[4 + N_COMMON:]
    cb_buf, sbz_ref, gaya_ref, sgb_ref = bufs[1:5]

    tile = jnp.minimum(pl.program_id(0), n_tiles - 1)
    t = tile % tiles_per_seq
    write_back = pltpu.make_async_copy(w_in_vmem, w_in_bf16_hbm, sems.at[SEM_WHOLE])

    @pl.when(pl.program_id(0) == 0)
    def _():
        stage = (sbz_ref, gaya_ref, sgb_ref)
        assert sum(r.shape[0] // STAGE_ROWS for r in stage) < SEM_WHOLE
        _convert_w_in(weights[0], w_in_vmem, stage, sems)
        _zero((slab_a, slab_b, cb_buf, sbz_ref, gaya_ref, sgb_ref))
        write_back.start()

    def store_y(val):
        y_ref[...] = val

    def step(**halves):
        convs = _PromptConvs(slab_a, slab_b, cb_buf, w_conv_a, w_conv_b, tm, t)
        _pipelined_step(lambda: xf_ref[...], lambda: xb_ref[...], lambda: pb_ref[...], store_y,
                        (w_in_vmem,) + tuple(weights[1:]), vecs, bufs, convs, **halves)

    _run_steps(n_tiles, step)

    @pl.when(pl.program_id(0) == n_tiles)
    def _():
        write_back.wait()

    @pl.when(t == tiles_per_seq - 1)
    def _():
        for c in range(N_SLAB):
            tail_a_ref[:, _lanes(c)] = slab_a[c, tm:tm + HIST_A, :]
            tail_b_ref[:, _lanes(c)] = slab_b[c, tm:tm + HIST_B, :]


class _SampleConvs:
    def __init__(self, st_a_ref, st_b_ref, s_ref, v_ref, cb_buf, w_conv_a, w_conv_b, ns, t_len):
        self.st_a_ref, self.st_b_ref, self.s_ref, self.v_ref = st_a_ref, st_b_ref, s_ref, v_ref
        self.cb_buf = cb_buf
        self.w_conv_a, self.w_conv_b, self.ns, self.t_len = w_conv_a, w_conv_b, ns, t_len

    def put_v(self, pieces):
        for c, piece in enumerate(pieces):
            self.v_ref[:, _lanes(c)] = piece

    def put_s(self, pieces):
        for c, piece in enumerate(pieces):
            self.s_ref[:, _lanes(c)] = piece

    def _conv(self, st_ref, new_ref, w_ref, k_taps, slabs=range(N_SLAB)):
        ns, t_len, hist = self.ns, self.t_len, k_taps - 1
        for c in slabs:
            w = [w_ref[k:k + 1, _lanes(c)] for k in range(k_taps)]
            for n0 in range(0, ns, SUBLANES):
                acc = [None] * t_len
                for e in range(hist + t_len):
                    if e < hist:
                        x = st_ref[e, n0:n0 + SUBLANES, _lanes(c)]
                    else:
                        r = (e - hist) * ns + n0
                        x = new_ref[r:r + SUBLANES, _lanes(c)]
                    for t in range(t_len):
                        k = e - t
                        if 0 <= k < k_taps:
                            term = x * w[k]
                            acc[t] = term if acc[t] is None else acc[t] + term
                for t in range(t_len):
                    self.cb_buf[t * ns + n0:t * ns + n0 + SUBLANES, _lanes(c)] = acc[t]

    def conv_a(self):
        self._conv(self.st_a_ref, self.s_ref, self.w_conv_a, K_A)

    def conv_b(self, slabs=range(N_SLAB)):
        self._conv(self.st_b_ref, self.v_ref, self.w_conv_b, K_B, slabs)


def _sample_kernel(*refs, ns, t_len, n_tiles):
    n_in = 5
    xf_ref, xb_ref, pb_ref, st_a_ref, st_b_ref = refs[:n_in]
    weights, (w_conv_a, w_conv_b), vecs, rest = _split_consts(refs[n_in:])
    y_ref, new_a_ref, new_b_ref = rest[:3]
    bufs, w_in_vmem, sems = rest[3:3 + N_BUFS], rest[3 + N_BUFS], rest[4 + N_BUFS]
    s_ref, v_ref = rest[3 + N_COMMON:]
    cb_buf, sbz_ref, gaya_ref, sgb_ref = bufs[1:5]
    rows = ns * t_len

    @pl.when(pl.program_id(0) == 0)
    def _():
        reload = pltpu.make_async_copy(weights[0], w_in_vmem, sems.at[SEM_WHOLE])
        reload.start()
        _zero((cb_buf, sbz_ref, gaya_ref, sgb_ref))
        reload.wait()

    def store_y(val):
        y_ref[...] = val.reshape(t_len, ns, D)

    def step(**halves):
        convs = _SampleConvs(st_a_ref, st_b_ref, s_ref, v_ref, cb_buf, w_conv_a, w_conv_b,
                             ns, t_len)
        _pipelined_step(lambda: xf_ref[...].reshape(rows, D),
                        lambda: xb_ref[...].reshape(rows, D),
                        lambda: pb_ref[...].reshape(rows, PLE), store_y,
                        (w_in_vmem,) + tuple(weights[1:]), vecs, bufs, convs, **halves)

    _run_steps(n_tiles, step)

    for e in range(K_A - 1):
        src = e + t_len - (K_A - 1)
        new_a_ref[e] = s_ref[src * ns:(src + 1) * ns, :]
    for e in range(K_B - 1):
        src = e + t_len
        if src < K_B - 1:
            new_b_ref[e] = st_b_ref[src]
        else:
            src -= K_B - 1
            new_b_ref[e] = v_ref[src * ns:(src + 1) * ns, :]


def _whole(memory_space=pltpu.VMEM):
    return pl.BlockSpec(memory_space=memory_space)


def kernel(x_prompt, x_sample, state_conv_a, state_conv_b, p_prompt, p_sample, g_mix, w_in, w_conv_a, w_out_a, w_conv_b, b_conv_b, ln_g, ln_b, w_out_b, w_o, w_pe, g_ple, w_pg, g_final):
    depth = w_in.shape[0]
    assert depth == 1, "single-layer trunk"
    n_p, seq, d = x_prompt.shape
    n_s, t_s, _ = x_sample.shape
    assert d == D and w_in.shape[2] == N_PROJ * D
    assert t_s >= K_A - 1 and t_s <= K_B - 1

    consts = (w_in[0],
              w_out_a[0].astype(BF16), w_out_b[0].astype(BF16),
              w_o[0].astype(BF16), w_pe[0].astype(BF16), w_pg[0].astype(BF16),
              w_conv_a[0], w_conv_b[0],
              g_mix[0][None], b_conv_b[0][None], ln_g[0][None], ln_b[0][None],
              g_ple[0][None], g_final[None])
    const_specs = [_whole(pl.ANY)] + [_whole() for _ in consts[1:]]
    params = pltpu.CompilerParams(dimension_semantics=("arbitrary",),
                                  vmem_limit_bytes=VMEM_LIMIT_BYTES)

    tm = 512
    assert seq % tm == 0 and tm >= HIST_B and tm % CONV_ROWS == 0
    tiles_per_seq = seq // tm
    n_tiles = n_p * tiles_per_seq

    def front(i):
        return (jnp.minimum(i, n_tiles - 1), 0)

    def back(i):
        return (jnp.maximum(i - 1, 0), 0)

    def seq_of_front(i):
        return (jnp.minimum(i, n_tiles - 1) // tiles_per_seq, 0, 0)

    x_flat = x_prompt.reshape(n_p * seq, D)
    y_p, tail_a, tail_b, w_in_bf16 = pl.pallas_call(
        functools.partial(_prompt_kernel, tm=tm, tiles_per_seq=tiles_per_seq, n_tiles=n_tiles),
        grid=(n_tiles + 1,),
        in_specs=[pl.BlockSpec((tm, D), front),
                  pl.BlockSpec((tm, D), back),
                  pl.BlockSpec((tm, PLE), back)] + const_specs,
        out_specs=[pl.BlockSpec((tm, D), back),
                   pl.BlockSpec((None, HIST_A, D), seq_of_front),
                   pl.BlockSpec((None, HIST_B, D), seq_of_front),
                   _whole(pl.ANY)],
        out_shape=[jax.ShapeDtypeStruct((n_p * seq, D), F32),
                   jax.ShapeDtypeStruct((n_p, HIST_A, D), F32),
                   jax.ShapeDtypeStruct((n_p, HIST_B, D), F32),
                   jax.ShapeDtypeStruct((D, N_PROJ * D), BF16)],
        scratch_shapes=_common_scratch(tm) + [
            pltpu.VMEM((N_SLAB, HIST_A + tm, LANES), F32),
            pltpu.VMEM((N_SLAB, HIST_B + tm, LANES), F32)],
        compiler_params=params,
        name="prompt_layer",
    )(x_flat, x_flat, p_prompt[0].reshape(n_p * seq, PLE), *consts)
    y_p = y_p.reshape(n_p, seq, D)
    new_a_p = tail_a[:, HIST_A - (K_A - 1):, :][None]
    new_b_p = tail_b[:, HIST_B - (K_B - 1):, :][None]

    ns = 32
    assert n_s % ns == 0 and ns % SUBLANES == 0
    n_tiles_s = n_s // ns

    def front_s(i):
        return (0, jnp.minimum(i, n_tiles_s - 1), 0)

    def back_s(i):
        return (0, jnp.maximum(i - 1, 0), 0)

    def t_major(a):
        return jnp.transpose(a, (1, 0, 2))

    x_t = t_major(x_sample)
    y_t, new_a_t, new_b_t = pl.pallas_call(
        functools.partial(_sample_kernel, ns=ns, t_len=t_s, n_tiles=n_tiles_s),
        grid=(n_tiles_s + 1,),
        in_specs=[pl.BlockSpec((t_s, ns, D), front_s),
                  pl.BlockSpec((t_s, ns, D), back_s),
                  pl.BlockSpec((t_s, ns, PLE), back_s),
                  pl.BlockSpec((K_A - 1, ns, D), front_s),
                  pl.BlockSpec((K_B - 1, ns, D), front_s)] + const_specs,
        out_specs=[pl.BlockSpec((t_s, ns, D), back_s),
                   pl.BlockSpec((K_A - 1, ns, D), front_s),
                   pl.BlockSpec((K_B - 1, ns, D), front_s)],
        out_shape=[jax.ShapeDtypeStruct((t_s, n_s, D), F32),
                   jax.ShapeDtypeStruct((K_A - 1, n_s, D), F32),
                   jax.ShapeDtypeStruct((K_B - 1, n_s, D), F32)],
        scratch_shapes=_common_scratch(ns * t_s) + [
            pltpu.VMEM((ns * t_s, D), F32),
            pltpu.VMEM((ns * t_s, D), F32)],
        compiler_params=params,
        name="sample_layer",
    )(x_t, x_t, t_major(p_sample[0]), t_major(state_conv_a[0]), t_major(state_conv_b[0]),
      w_in_bf16, *consts[1:])
    return (y_p, t_major(y_t), new_a_p, new_b_p, t_major(new_a_t)[None], t_major(new_b_t)[None])
```

```python
import functools

import jax
import jax.numpy as jnp
from jax import lax
from jax.experimental import pallas as pl
from jax.experimental.pallas import tpu as pltpu

D = 1024
N_PROJ = 9
PLE = 256
K_A = 3
K_B = 31
EPS = 1e-6
LN_EPS = 1e-5
LANES = 128
SUBLANES = 8
N_SLAB = D // LANES
HIST_A = SUBLANES
HIST_B = -(-(K_B - 1) // SUBLANES) * SUBLANES
CONV_ROWS = 64
STAGE_ROWS = 128
N_SEMS = 16
SEM_WHOLE = N_SEMS - 1
VMEM_LIMIT_BYTES = 63 * 1024 * 1024

P_AB, P_AC, P_AH, P_AZ, P_BV, P_BG, P_BZ, P_GA, P_GB = range(N_PROJ)
PAIRS = ((P_BV, P_BG), (P_AC, P_AH), (P_AB, P_AZ))
SINGLES = (P_BZ, P_GA, P_GB)
W_IN_ORDER = (tuple((j, c) for pair in PAIRS for c in range(D // 128) for j in pair)
              + tuple((j, c) for j in SINGLES for c in range(D // 128)))

F32 = jnp.float32
BF16 = jnp.bfloat16


def _sigmoid(x):
    return 0.5 * jnp.tanh(0.5 * x) + 0.5


def _silu(x):
    h = 0.5 * x
    return h * jnp.tanh(h) + h


def _rmsnorm(x, g):
    ms = jnp.mean(x * x, axis=-1, keepdims=True)
    return x * lax.rsqrt(ms + EPS) * g


def _layernorm(x, g, b):
    mu = jnp.mean(x, axis=-1, keepdims=True)
    xc = x - mu
    var = jnp.mean(xc * xc, axis=-1, keepdims=True)
    return xc * lax.rsqrt(var + LN_EPS) * g + b


def _lanes(c):
    return slice(c * LANES, (c + 1) * LANES)


def _pipelined_step(load_xf, load_xb, load_pb, store_y, weights, vecs, bufs, convs,
                    do_front=True, do_tail=True, ready=lambda j: None):
    w_in, w_out_a, w_out_b, w_o, w_pe, w_pg = weights
    g_mix, b_conv_b, ln_g, ln_b, g_ple, g_final = vecs
    u_ref, cb_buf, sbz_ref, gaya_ref, sgb_ref, ybpre_ref, yapre_ref = bufs

    def dot(a, w):
        return jnp.dot(a, w, preferred_element_type=F32)

    def proj(j):
        ready(j)
        col = (2 * len(PAIRS) + SINGLES.index(j)) * D
        return dot(u_ref[...], w_in[:, col:col + D])

    def proj_pair(j1, j2):
        ready(j2)
        col = PAIRS.index((j1, j2)) * 2 * D
        r = dot(u_ref[...], w_in[:, col:col + 2 * D])
        return [(r[:, 2 * c * LANES:(2 * c + 1) * LANES],
                 r[:, (2 * c + 1) * LANES:(2 * c + 2) * LANES]) for c in range(N_SLAB)]

    if do_front:
        u_ref[...] = _rmsnorm(load_xf(), g_mix[...]).astype(BF16)

    if do_tail:
        cb = cb_buf[...] + b_conv_b[...]
        yb_pre = _silu(_layernorm(cb, ln_g[...], ln_b[...])) * sbz_ref[...]
        ybpre_ref[...] = yb_pre.astype(BF16)

    if do_front:
        convs.put_v([bv * _sigmoid(bg) for bv, bg in proj_pair(P_BV, P_BG)])
        convs.put_s([ac * ah for ac, ah in proj_pair(P_AC, P_AH)])
        convs.conv_a()
        for c, (ab, az) in enumerate(proj_pair(P_AB, P_AZ)):
            yapre_ref[:, _lanes(c)] = (ab * cb_buf[:, _lanes(c)] * _silu(az)).astype(BF16)
        ya = dot(yapre_ref[...], w_out_a[...])

    if do_tail:
        yb = dot(ybpre_ref[...], w_out_b[...])
        m = gaya_ref[...] + sgb_ref[...] * yb
        h = load_xb() + dot(m.astype(BF16), w_o[...])
        pe = dot(load_pb().astype(BF16), w_pe[...])
        r = _rmsnorm(h, g_ple[...]).astype(BF16)
        pg = _sigmoid(dot(r, w_pg[...]))
        h = h + pg * pe
        store_y(_rmsnorm(h, g_final[...]))

    if do_front:
        convs.conv_b(range(0, N_SLAB // 2))
        sbz_ref[...] = _silu(proj(P_BZ))
        gaya_ref[...] = _sigmoid(proj(P_GA)) * ya
        sgb_ref[...] = _sigmoid(proj(P_GB))
        convs.conv_b(range(N_SLAB // 2, N_SLAB))


def _run_steps(n_tiles, step, first=None):
    i = pl.program_id(0)

    if first is not None:
        pl.when(i == 0)(first)

    @pl.when((i < n_tiles) if first is None else jnp.logical_and(i > 0, i < n_tiles))
    def _():
        step()

    @pl.when(i == n_tiles)
    def _():
        step(do_front=False)


USE_ORDER = tuple(j for pair in PAIRS for j in pair) + SINGLES


class _WInConverter:
    def __init__(self, w_f32, w_vmem, stage_refs, sems):
        self.w_f32, self.w_vmem, self.sems = w_f32, w_vmem, sems
        self.slots = [(ref, r0) for ref in stage_refs
                      for r0 in range(0, ref.shape[0], STAGE_ROWS)]
        self.chunks = [(j, r0) for j in USE_ORDER for r0 in range(0, D, STAGE_ROWS)]
        self.pos_of = {jc: pos for pos, jc in enumerate(W_IN_ORDER)}
        self.done = 0

    def _copy(self, k):
        j, r0 = self.chunks[k]
        ref, s0 = self.slots[k % len(self.slots)]
        return pltpu.make_async_copy(self.w_f32.at[pl.ds(r0, STAGE_ROWS), pl.ds(j * D, D)],
                                     ref.at[pl.ds(s0, STAGE_ROWS), :],
                                     self.sems.at[k % len(self.slots)])

    def start(self):
        for k in range(len(self.slots)):
            self._copy(k).start()

    def through(self, j):
        last = USE_ORDER[-1] if j in SINGLES else j
        end = (USE_ORDER.index(last) + 1) * (D // STAGE_ROWS)
        while self.done < end:
            k = self.done
            jk, r0 = self.chunks[k]
            self._copy(k).wait()
            ref, s0 = self.slots[k % len(self.slots)]
            for c in range(N_SLAB):
                self.w_vmem[r0:r0 + STAGE_ROWS, _lanes(self.pos_of[(jk, c)])] = (
                    ref[s0:s0 + STAGE_ROWS, _lanes(c)].astype(BF16))
            if k + len(self.slots) < len(self.chunks):
                self._copy(k + len(self.slots)).start()
            self.done += 1


def _zero(refs):
    for ref in refs:
        ref[...] = jnp.zeros(ref.shape, ref.dtype)


def _common_scratch(tm):
    return [pltpu.VMEM((tm, D), BF16),
            pltpu.VMEM((tm, D), F32),
            pltpu.VMEM((tm, D), F32),
            pltpu.VMEM((tm, D), F32),
            pltpu.VMEM((tm, D), F32),
            pltpu.VMEM((tm, D), BF16),
            pltpu.VMEM((tm, D), BF16),
            pltpu.VMEM((D, N_PROJ * D), BF16),
            pltpu.SemaphoreType.DMA((N_SEMS,))]


N_BUFS = 7
N_COMMON = 9
N_WEIGHTS = 6
N_VECS = 6


def _split_consts(refs):
    a, b, c = N_WEIGHTS, N_WEIGHTS + 2, N_WEIGHTS + 2 + N_VECS
    return refs[:a], refs[a:b], refs[b:c], refs[c:]


class _PromptConvs:
    def __init__(self, slab_a, slab_b, cb_buf, w_conv_a, w_conv_b, tm, t):
        self.slab_a, self.slab_b, self.cb_buf = slab_a, slab_b, cb_buf
        self.w_conv_a, self.w_conv_b, self.tm, self.t = w_conv_a, w_conv_b, tm, t

    def _put(self, slab, hist, pieces):
        prev = slab[:, self.tm:self.tm + hist, :]
        slab[:, 0:hist, :] = jnp.where(self.t == 0, 0.0, prev)
        for c, piece in enumerate(pieces):
            slab[c, hist:hist + self.tm, :] = piece

    def put_v(self, v):
        self._put(self.slab_b, HIST_B, v)

    def put_s(self, s):
        self._put(self.slab_a, HIST_A, s)

    def _conv(self, slab, w_ref, hist, k_taps, slabs=range(N_SLAB)):
        first = hist - (k_taps - 1)
        for c in slabs:
            for r0 in range(0, self.tm, CONV_ROWS):
                acc = None
                for k in range(k_taps):
                    start = first + r0 + k
                    term = slab[c, start:start + CONV_ROWS, :] * w_ref[k:k + 1, _lanes(c)]
                    acc = term if acc is None else acc + term
                self.cb_buf[r0:r0 + CONV_ROWS, _lanes(c)] = acc

    def conv_a(self):
        self._conv(self.slab_a, self.w_conv_a, HIST_A, K_A)

    def conv_b(self, slabs=range(N_SLAB)):
        self._conv(self.slab_b, self.w_conv_b, HIST_B, K_B, slabs)


def _prompt_kernel(*refs, tm, tiles_per_seq, n_tiles):
    n_in = 3
    xf_ref, xb_ref, pb_ref = refs[:n_in]
    weights, (w_conv_a, w_conv_b), vecs, rest = _split_consts(refs[n_in:])
    y_ref, tail_a_ref, tail_b_ref, w_in_bf16_hbm = rest[:4]
    bufs, w_in_vmem, sems = rest[4:4 + N_BUFS], rest[4 + N_BUFS], rest[5 + N_BUFS]
    slab_a, slab_b = rest[4 + N_COMMON:]
    cb_buf, sbz_ref, gaya_ref, sgb_ref = bufs[1:5]

    tile = jnp.minimum(pl.program_id(0), n_tiles - 1)
    t = tile % tiles_per_seq
    write_back = pltpu.make_async_copy(w_in_vmem, w_in_bf16_hbm, sems.at[SEM_WHOLE])


    def store_y(val):
        y_ref[...] = val

    def step(**halves):
        convs = _PromptConvs(slab_a, slab_b, cb_buf, w_conv_a, w_conv_b, tm, t)
        _pipelined_step(lambda: xf_ref[...], lambda: xb_ref[...], lambda: pb_ref[...], store_y,
                        (w_in_vmem,) + tuple(weights[1:]), vecs, bufs, convs, **halves)

    def first_step():
        stage = (sbz_ref, gaya_ref, sgb_ref)
        assert sum(r.shape[0] // STAGE_ROWS for r in stage) < SEM_WHOLE
        w_in_conv = _WInConverter(weights[0], w_in_vmem, stage, sems)
        w_in_conv.start()
        slab_a[:, tm:tm + HIST_A, :] = jnp.zeros((N_SLAB, HIST_A, LANES), F32)
        slab_b[:, tm:tm + HIST_B, :] = jnp.zeros((N_SLAB, HIST_B, LANES), F32)
        step(do_tail=False, ready=w_in_conv.through)
        write_back.start()

    _run_steps(n_tiles, step, first=first_step)

    @pl.when(pl.program_id(0) == n_tiles)
    def _():
        write_back.wait()

    @pl.when(t == tiles_per_seq - 1)
    def _():
        for c in range(N_SLAB):
            tail_a_ref[:, _lanes(c)] = slab_a[c, tm:tm + HIST_A, :]
            tail_b_ref[:, _lanes(c)] = slab_b[c, tm:tm + HIST_B, :]


class _SampleConvs:
    def __init__(self, st_a_ref, st_b_ref, s_ref, v_ref, cb_buf, w_conv_a, w_conv_b, ns, t_len):
        self.st_a_ref, self.st_b_ref, self.s_ref, self.v_ref = st_a_ref, st_b_ref, s_ref, v_ref
        self.cb_buf = cb_buf
        self.w_conv_a, self.w_conv_b, self.ns, self.t_len = w_conv_a, w_conv_b, ns, t_len

    def put_v(self, pieces):
        for c, piece in enumerate(pieces):
            self.v_ref[:, _lanes(c)] = piece

    def put_s(self, pieces):
        for c, piece in enumerate(pieces):
            self.s_ref[:, _lanes(c)] = piece

    def _conv(self, st_ref, new_ref, w_ref, k_taps, slabs=range(N_SLAB)):
        ns, t_len, hist = self.ns, self.t_len, k_taps - 1
        for c in slabs:
            w = [w_ref[k:k + 1, _lanes(c)] for k in range(k_taps)]
            for n0 in range(0, ns, SUBLANES):
                acc = [None] * t_len
                for e in range(hist + t_len):
                    if e < hist:
                        x = st_ref[e, n0:n0 + SUBLANES, _lanes(c)]
                    else:
                        r = (e - hist) * ns + n0
                        x = new_ref[r:r + SUBLANES, _lanes(c)]
                    for t in range(t_len):
                        k = e - t
                        if 0 <= k < k_taps:
                            term = x * w[k]
                            acc[t] = term if acc[t] is None else acc[t] + term
                for t in range(t_len):
                    self.cb_buf[t * ns + n0:t * ns + n0 + SUBLANES, _lanes(c)] = acc[t]

    def conv_a(self):
        self._conv(self.st_a_ref, self.s_ref, self.w_conv_a, K_A)

    def conv_b(self, slabs=range(N_SLAB)):
        self._conv(self.st_b_ref, self.v_ref, self.w_conv_b, K_B, slabs)


def _sample_kernel(*refs, ns, t_len, n_tiles):
    n_in = 5
    xf_ref, xb_ref, pb_ref, st_a_ref, st_b_ref = refs[:n_in]
    weights, (w_conv_a, w_conv_b), vecs, rest = _split_consts(refs[n_in:])
    y_ref, new_a_ref, new_b_ref = rest[:3]
    bufs, w_in_vmem, sems = rest[3:3 + N_BUFS], rest[3 + N_BUFS], rest[4 + N_BUFS]
    s_ref, v_ref = rest[3 + N_COMMON:]
    cb_buf, sbz_ref, gaya_ref, sgb_ref = bufs[1:5]
    rows = ns * t_len

    @pl.when(pl.program_id(0) == 0)
    def _():
        reload = pltpu.make_async_copy(weights[0], w_in_vmem, sems.at[SEM_WHOLE])
        reload.start()
        _zero((cb_buf, sbz_ref, gaya_ref, sgb_ref))
        reload.wait()

    def store_y(val):
        y_ref[...] = val.reshape(t_len, ns, D)

    def step(**halves):
        convs = _SampleConvs(st_a_ref, st_b_ref, s_ref, v_ref, cb_buf, w_conv_a, w_conv_b,
                             ns, t_len)
        _pipelined_step(lambda: xf_ref[...].reshape(rows, D),
                        lambda: xb_ref[...].reshape(rows, D),
                        lambda: pb_ref[...].reshape(rows, PLE), store_y,
                        (w_in_vmem,) + tuple(weights[1:]), vecs, bufs, convs, **halves)

    _run_steps(n_tiles, step)

    for e in range(K_A - 1):
        src = e + t_len - (K_A - 1)
        new_a_ref[e] = s_ref[src * ns:(src + 1) * ns, :]
    for e in range(K_B - 1):
        src = e + t_len
        if src < K_B - 1:
            new_b_ref[e] = st_b_ref[src]
        else:
            src -= K_B - 1
            new_b_ref[e] = v_ref[src * ns:(src + 1) * ns, :]


def _whole(memory_space=pltpu.VMEM):
    return pl.BlockSpec(memory_space=memory_space)


def kernel(x_prompt, x_sample, state_conv_a, state_conv_b, p_prompt, p_sample, g_mix, w_in, w_conv_a, w_out_a, w_conv_b, b_conv_b, ln_g, ln_b, w_out_b, w_o, w_pe, g_ple, w_pg, g_final):
    depth = w_in.shape[0]
    assert depth == 1, "single-layer trunk"
    n_p, seq, d = x_prompt.shape
    n_s, t_s, _ = x_sample.shape
    assert d == D and w_in.shape[2] == N_PROJ * D
    assert t_s >= K_A - 1 and t_s <= K_B - 1

    consts = (w_in[0],
              w_out_a[0].astype(BF16), w_out_b[0].astype(BF16),
              w_o[0].astype(BF16), w_pe[0].astype(BF16), w_pg[0].astype(BF16),
              w_conv_a[0], w_conv_b[0],
              g_mix[0][None], b_conv_b[0][None], ln_g[0][None], ln_b[0][None],
              g_ple[0][None], g_final[None])
    const_specs = [_whole(pl.ANY)] + [_whole() for _ in consts[1:]]
    params = pltpu.CompilerParams(dimension_semantics=("arbitrary",),
                                  vmem_limit_bytes=VMEM_LIMIT_BYTES)

    tm = 512
    assert seq % tm == 0 and tm >= HIST_B and tm % CONV_ROWS == 0
    tiles_per_seq = seq // tm
    n_tiles = n_p * tiles_per_seq

    def front(i):
        return (jnp.minimum(i, n_tiles - 1), 0)

    def back(i):
        return (jnp.maximum(i - 1, 0), 0)

    def seq_of_front(i):
        return (jnp.minimum(i, n_tiles - 1) // tiles_per_seq, 0, 0)

    x_flat = x_prompt.reshape(n_p * seq, D)
    y_p, tail_a, tail_b, w_in_bf16 = pl.pallas_call(
        functools.partial(_prompt_kernel, tm=tm, tiles_per_seq=tiles_per_seq, n_tiles=n_tiles),
        grid=(n_tiles + 1,),
        in_specs=[pl.BlockSpec((tm, D), front),
                  pl.BlockSpec((tm, D), back),
                  pl.BlockSpec((tm, PLE), back)] + const_specs,
        out_specs=[pl.BlockSpec((tm, D), back),
                   pl.BlockSpec((None, HIST_A, D), seq_of_front),
                   pl.BlockSpec((None, HIST_B, D), seq_of_front),
                   _whole(pl.ANY)],
        out_shape=[jax.ShapeDtypeStruct((n_p * seq, D), F32),
                   jax.ShapeDtypeStruct((n_p, HIST_A, D), F32),
                   jax.ShapeDtypeStruct((n_p, HIST_B, D), F32),
                   jax.ShapeDtypeStruct((D, N_PROJ * D), BF16)],
        scratch_shapes=_common_scratch(tm) + [
            pltpu.VMEM((N_SLAB, HIST_A + tm, LANES), F32),
            pltpu.VMEM((N_SLAB, HIST_B + tm, LANES), F32)],
        compiler_params=params,
        name="prompt_layer",
    )(x_flat, x_flat, p_prompt[0].reshape(n_p * seq, PLE), *consts)
    y_p = y_p.reshape(n_p, seq, D)
    new_a_p = tail_a[:, HIST_A - (K_A - 1):, :][None]
    new_b_p = tail_b[:, HIST_B - (K_B - 1):, :][None]

    ns = 32
    assert n_s % ns == 0 and ns % SUBLANES == 0
    n_tiles_s = n_s // ns

    def front_s(i):
        return (0, jnp.minimum(i, n_tiles_s - 1), 0)

    def back_s(i):
        return (0, jnp.maximum(i - 1, 0), 0)

    def t_major(a):
        return jnp.transpose(a, (1, 0, 2))

    x_t = t_major(x_sample)
    y_t, new_a_t, new_b_t = pl.pallas_call(
        functools.partial(_sample_kernel, ns=ns, t_len=t_s, n_tiles=n_tiles_s),
        grid=(n_tiles_s + 1,),
        in_specs=[pl.BlockSpec((t_s, ns, D), front_s),
                  pl.BlockSpec((t_s, ns, D), back_s),
                  pl.BlockSpec((t_s, ns, PLE), back_s),
                  pl.BlockSpec((K_A - 1, ns, D), front_s),
                  pl.BlockSpec((K_B - 1, ns, D), front_s)] + const_specs,
        out_specs=[pl.BlockSpec((t_s, ns, D), back_s),
                   pl.BlockSpec((K_A - 1, ns, D), front_s),
                   pl.BlockSpec((K_B - 1, ns, D), front_s)],
        out_shape=[jax.ShapeDtypeStruct((t_s, n_s, D), F32),
                   jax.ShapeDtypeStruct((K_A - 1, n_s, D), F32),
                   jax.ShapeDtypeStruct((K_B - 1, n_s, D), F32)],
        scratch_shapes=_common_scratch(ns * t_s) + [
            pltpu.VMEM((ns * t_s, D), F32),
            pltpu.VMEM((ns * t_s, D), F32)],
        compiler_params=params,
        name="sample_layer",
    )(x_t, x_t, t_major(p_sample[0]), t_major(state_conv_a[0]), t_major(state_conv_b[0]),
      w_in_bf16, *consts[1:])
    return (y_p, t_major(y_t), new_a_p, new_b_p, t_major(new_a_t)[None], t_major(new_b_t)[None])
```
